```python
import jax, jax.numpy as jnp
from jax import lax
import numpy as np

D_MODEL = 1024
BATCH = 32
SEQ = 256
DEPTH = 2
DEC_BATCH = 4
DEC_SEQ = 4096
PAST_LEN = 256

GRID_W = 64
N_HEADS = 8
N_KV = 2
HEAD_DIM = 64
GROUP = N_HEADS // N_KV
ATT_W = N_HEADS * HEAD_DIM
KV_W = N_KV * HEAD_DIM
WINDOW = 128
BLOCK = 128
ROPE_BASE = 10000.0
CONV_W = D_MODEL // 4
CONV_K = 31
POOL_W = D_MODEL // 4
POOL_SIZES = (2, 4, 8, 16)
N_POOL_G = 4
POOL_G = POOL_W // N_POOL_G
SC_W = D_MODEL // 4
SC_K = 3
N_BRANCH = 4
D_FF = 4 * D_MODEL
EPS = 1e-6
NEG = -1e30
SPLIT_SIZES = (CONV_W, CONV_W, POOL_W, SC_W, SC_W, SC_W, ATT_W, KV_W, KV_W, N_BRANCH * D_MODEL)
IN_W = 2 * CONV_W + POOL_W + 3 * SC_W + ATT_W + 2 * KV_W + N_BRANCH * D_MODEL

kernel_name = "hybrid_diffusion_parallel_mixer_step"


def _rmsnorm(x, g):
    xf = x.astype(jnp.float32)
    y = xf * lax.rsqrt(jnp.mean(xf * xf, axis=-1, keepdims=True) + EPS)
    return y.astype(x.dtype) * g


def _layernorm(x, g, b):
    xf = x.astype(jnp.float32)
    mu = jnp.mean(xf, axis=-1, keepdims=True)
    var = jnp.mean(jnp.square(xf - mu), axis=-1, keepdims=True)
    y = (xf - mu) * lax.rsqrt(var + EPS)
    return y.astype(x.dtype) * g + b


def _dwconv(x, w):
    pad = w.shape[0] // 2
    return lax.conv_general_dilated(x, w[:, None, :], window_strides=(1,), padding=[(pad, pad)],
                                    dimension_numbers=('NWC', 'WIO', 'NWC'),
                                    feature_group_count=x.shape[-1])


def _multiscale_pool(u):
    bsz, n, _ = u.shape
    ug = u.astype(jnp.float32).reshape(bsz, n, N_POOL_G, POOL_G)
    cs = jnp.concatenate([jnp.zeros((bsz, 1, N_POOL_G, POOL_G), jnp.float32), lax.cumsum(ug, axis=1)], axis=1)
    t = jnp.arange(n)
    outs = []
    for gi, w in enumerate(POOL_SIZES):
        lo = jnp.clip(t - w // 2, 0, n)
        hi = jnp.clip(t + w // 2, 0, n)
        cnt = (hi - lo).astype(jnp.float32)[None, :, None]
        outs.append((cs[:, hi, gi] - cs[:, lo, gi]) / cnt - ug[:, :, gi])
    return jnp.stack(outs, axis=2).astype(u.dtype)


def _rope_1d(x, pos):
    half = x.shape[-1] // 2
    freq = ROPE_BASE ** (-jnp.arange(half, dtype=jnp.float32) / half)
    ang = pos.astype(jnp.float32)[:, None] * freq[None, :]
    cos = jnp.cos(ang)[None, :, None, :]
    sin = jnp.sin(ang)[None, :, None, :]
    xf = x.astype(jnp.float32)
    x1, x2 = xf[..., :half], xf[..., half:]
    return jnp.concatenate([x1 * cos - x2 * sin, x2 * cos + x1 * sin], axis=-1).astype(x.dtype)


def _rope_2d(x):
    n = x.shape[1]
    rows = n // GRID_W
    row = jnp.repeat(jnp.arange(rows), GRID_W)
    col = jnp.tile(jnp.arange(GRID_W), rows)
    r = x.shape[-1] // 2
    return jnp.concatenate([_rope_1d(x[..., :r], row), _rope_1d(x[..., r:], col)], axis=-1)


def _context_attention(q, k, v, sink):
    bsz, n = q.shape[:2]
    nbq = n // BLOCK
    scale = HEAD_DIM ** -0.5
    qb = q.reshape(bsz, nbq, BLOCK, N_KV, GROUP, HEAD_DIM).transpose(1, 0, 2, 3, 4, 5)
    sink_f = sink.astype(jnp.float32).reshape(1, N_KV, GROUP, 1, 1)

    def one_block(qblk):
        s = jnp.einsum('bqhgd,bchd->bhgqc', qblk, k).astype(jnp.float32) * scale
        sk = jnp.broadcast_to(sink_f, s.shape[:-1] + (1,))
        p = jax.nn.softmax(jnp.concatenate([s, sk], axis=-1), axis=-1)[..., :-1]
        return jnp.einsum('bhgqc,bchd->bqhgd', p.astype(v.dtype), v)

    o = lax.map(one_block, qb)
    return o.transpose(1, 0, 2, 3, 4, 5).reshape(bsz, n, ATT_W)


def _latent_attention(q, k, v, kc, vc, sink):
    bsz, n = q.shape[:2]
    nb = n // BLOCK
    scale = HEAD_DIM ** -0.5
    qb = q.reshape(bsz, nb, BLOCK, N_KV, GROUP, HEAD_DIM)
    pad = ((0, 0), (BLOCK, BLOCK), (0, 0), (0, 0))
    kp = jnp.pad(k, pad)
    vp = jnp.pad(v, pad)
    idx = jnp.arange(nb)[:, None] * BLOCK + jnp.arange(3 * BLOCK)[None, :]
    kl = kp[:, idx]
    vl = vp[:, idx]
    kpos = idx - BLOCK
    qpos = jnp.arange(nb)[:, None] * BLOCK + jnp.arange(BLOCK)[None, :]
    valid = ((jnp.abs(qpos[:, :, None] - kpos[:, None, :]) <= WINDOW)
             & (kpos[:, None, :] >= 0) & (kpos[:, None, :] < n))
    s_loc = jnp.einsum('bnqhgd,bnshd->bnhgqs', qb, kl).astype(jnp.float32) * scale
    s_loc = jnp.where(valid[None, :, None, None], s_loc, NEG)
    s_ctx = jnp.einsum('bnqhgd,bchd->bnhgqc', qb, kc).astype(jnp.float32) * scale
    sk = jnp.broadcast_to(sink.astype(jnp.float32).reshape(1, 1, N_KV, GROUP, 1, 1), s_loc.shape[:-1] + (1,))
    p = jax.nn.softmax(jnp.concatenate([s_loc, s_ctx, sk], axis=-1), axis=-1)
    p_loc = p[..., :3 * BLOCK].astype(v.dtype)
    p_ctx = p[..., 3 * BLOCK:-1].astype(vc.dtype)
    o = (jnp.einsum('bnhgqs,bnshd->bnqhgd', p_loc, vl)
         + jnp.einsum('bnhgqc,bchd->bnqhgd', p_ctx, vc))
    return o.reshape(bsz, n, ATT_W)


def _mixer(h, w_in, w_conv_a, b_conv_a, ln_g_a, ln_b_a, w_a_out, w_pool, pool_scale, w_b_out,
           w_sc, w_c_out, sink, w_d_out, w_o, ctx_k, ctx_v):
    bsz, n, _ = h.shape
    offs = np.cumsum(SPLIT_SIZES)[:-1].tolist()
    a_val, a_gate, u_pool, s_b, s_c, s_h, q, k, v, g_br = jnp.split(h @ w_in, offs, axis=-1)
    ua = _dwconv(a_val * jax.nn.sigmoid(a_gate), w_conv_a) + b_conv_a
    br_a = jax.nn.silu(_layernorm(ua, ln_g_a, ln_b_a)) @ w_a_out
    pooled = _multiscale_pool(u_pool)
    ub = jnp.einsum('bngc,gcd->bngd', pooled, w_pool).reshape(bsz, n, POOL_W) * pool_scale
    br_b = ub @ w_b_out
    br_c = (s_b * _dwconv(s_c * s_h, w_sc)) @ w_c_out
    q = q.reshape(bsz, n, N_HEADS, HEAD_DIM)
    k = k.reshape(bsz, n, N_KV, HEAD_DIM)
    v = v.reshape(bsz, n, N_KV, HEAD_DIM)
    if ctx_k is None:
        o = _context_attention(q, k, v, sink)
    else:
        o = _latent_attention(_rope_2d(q), _rope_2d(k), v, ctx_k, ctx_v, sink)
    br_d = o @ w_d_out
    g = jax.nn.sigmoid(g_br).reshape(bsz, n, N_BRANCH, D_MODEL)
    merged = g[:, :, 0] * br_a + g[:, :, 1] * br_b + g[:, :, 2] * br_c + g[:, :, 3] * br_d
    return merged @ w_o, k, v


def _layer(x, cond, w_ada, b_ada, g_pre_mix, g_post_mix, g_pre_ffn, g_post_ffn, w_in, w_conv_a, b_conv_a,
           ln_g_a, ln_b_a, w_a_out, w_pool, pool_scale, w_b_out, w_sc, w_c_out, sink, w_d_out, w_o,
           w_ff1, w_ff2, ctx_k, ctx_v):
    mod = jax.nn.silu(cond) @ w_ada + b_ada
    sh1, sc1, gt1, sh2, sc2, gt2 = jnp.split(mod[:, None, :], 6, axis=-1)
    h = _rmsnorm(x, g_pre_mix) * (1 + sc1) + sh1
    m, k, v = _mixer(h, w_in, w_conv_a, b_conv_a, ln_g_a, ln_b_a, w_a_out, w_pool, pool_scale, w_b_out,
                     w_sc, w_c_out, sink, w_d_out, w_o, ctx_k, ctx_v)
    x = x + gt1 * _rmsnorm(m, g_post_mix)
    h = _rmsnorm(x, g_pre_ffn) * (1 + sc2) + sh2
    f = jnp.square(jax.nn.relu(h @ w_ff1)) @ w_ff2
    x = x + gt2 * _rmsnorm(f, g_post_ffn)
    return x, k, v


def setup_inputs(seed: int = 0) -> dict:
    key = jax.random.key(seed)
    ks = jax.random.split(key, 32)
    f32 = jnp.float32

    def nrm(i, shape, scale=1.0):
        return jax.random.normal(ks[i], shape, f32) * scale

    L = DEPTH
    return {
        'x_prompt': nrm(0, (BATCH, SEQ, D_MODEL)),
        'x_sample': nrm(1, (DEC_BATCH, DEC_SEQ, D_MODEL)),
        'cache_k': nrm(2, (DEC_BATCH, DEPTH, PAST_LEN, N_KV, HEAD_DIM)),
        'cache_v': nrm(3, (DEC_BATCH, DEPTH, PAST_LEN, N_KV, HEAD_DIM)),
        'c': nrm(4, (DEC_BATCH, D_MODEL)),
        'c_ctx': nrm(5, (D_MODEL,)),
        'w_ada': nrm(6, (L, D_MODEL, 6 * D_MODEL), 0.5 * D_MODEL ** -0.5),
        'b_ada': nrm(7, (L, 6 * D_MODEL), 0.02),
        'g_pre_mix': 1.0 + nrm(8, (L, D_MODEL), 0.05),
        'g_post_mix': 1.0 + nrm(9, (L, D_MODEL), 0.05),
        'g_pre_ffn': 1.0 + nrm(10, (L, D_MODEL), 0.05),
        'g_post_ffn': 1.0 + nrm(11, (L, D_MODEL), 0.05),
        'w_in': nrm(12, (L, D_MODEL, IN_W), D_MODEL ** -0.5),
        'w_conv_a': nrm(13, (L, CONV_K, CONV_W), CONV_K ** -0.5),
        'b_conv_a': nrm(14, (L, CONV_W), 0.02),
        'ln_g_a': 1.0 + nrm(15, (L, CONV_W), 0.05),
        'ln_b_a': nrm(16, (L, CONV_W), 0.02),
        'w_a_out': nrm(17, (L, CONV_W, D_MODEL), CONV_W ** -0.5),
        'w_pool': nrm(18, (L, N_POOL_G, POOL_G, POOL_G), POOL_G ** -0.5),
        'pool_scale': 1.0 + nrm(19, (L, POOL_W), 0.1),
        'w_b_out': nrm(20, (L, POOL_W, D_MODEL), POOL_W ** -0.5),
        'w_sc': nrm(21, (L, SC_K, SC_W), SC_K ** -0.5),
        'w_c_out': nrm(22, (L, SC_W, D_MODEL), SC_W ** -0.5),
        'sink': nrm(23, (L, N_HEADS), 0.5),
        'w_d_out': nrm(24, (L, ATT_W, D_MODEL), ATT_W ** -0.5),
        'w_o': nrm(25, (L, D_MODEL, D_MODEL), D_MODEL ** -0.5),
        'w_ff1': nrm(26, (L, D_MODEL, D_FF), D_MODEL ** -0.5),
        'w_ff2': nrm(27, (L, D_FF, D_MODEL), D_FF ** -0.5),
    }


def reference(x_prompt, x_sample, cache_k, cache_v, c, c_ctx, w_ada, b_ada, g_pre_mix, g_post_mix,
              g_pre_ffn, g_post_ffn, w_in, w_conv_a, b_conv_a, ln_g_a, ln_b_a, w_a_out, w_pool,
              pool_scale, w_b_out, w_sc, w_c_out, sink, w_d_out, w_o, w_ff1, w_ff2):
    def lw(l):
        return (w_ada[l], b_ada[l], g_pre_mix[l], g_post_mix[l], g_pre_ffn[l], g_post_ffn[l], w_in[l],
                w_conv_a[l], b_conv_a[l], ln_g_a[l], ln_b_a[l], w_a_out[l], w_pool[l], pool_scale[l],
                w_b_out[l], w_sc[l], w_c_out[l], sink[l], w_d_out[l], w_o[l], w_ff1[l], w_ff2[l])

    xp = x_prompt
    cond_ctx = c_ctx[None, :]
    ks, vs = [], []
    for l in range(DEPTH):
        xp, k, v = _layer(xp, cond_ctx, *lw(l), None, None)
        ks.append(k)
        vs.append(v)
    new_k = jnp.stack(ks, axis=1)
    new_v = jnp.stack(vs, axis=1)

    xs = x_sample
    for l in range(DEPTH):
        xs, _, _ = _layer(xs, c, *lw(l), cache_k[:, l], cache_v[:, l])

    return (xp, xs, new_k, new_v)
```

```python
import functools

import jax
import jax.numpy as jnp
from jax import lax
from jax.experimental import pallas as pl
from jax.experimental.pallas import tpu as pltpu

F32 = jnp.float32
BF16 = jnp.bfloat16

D_MODEL = 1024
DEPTH = 2
GRID_W = 64
N_HEADS = 8
N_KV = 2
HEAD_DIM = 64
GROUP = N_HEADS // N_KV
ATT_W = N_HEADS * HEAD_DIM
KV_W = N_KV * HEAD_DIM
WINDOW = 128
ROPE_BASE = 10000.0
CONV_W = D_MODEL // 4
CONV_K = 31
POOL_W = D_MODEL // 4
POOL_SIZES = (2, 4, 8, 16)
POOL_G = POOL_W // len(POOL_SIZES)
SC_W = D_MODEL // 4
SC_K = 3
N_BRANCH = 4
D_FF = 4 * D_MODEL
EPS = 1e-6
NEG = -1e30
IN_W = 2 * CONV_W + POOL_W + 3 * SC_W + ATT_W + 2 * KV_W + N_BRANCH * D_MODEL

OFF_AVAL = 0
OFF_AGATE = OFF_AVAL + CONV_W
OFF_POOL = OFF_AGATE + CONV_W
OFF_SB = OFF_POOL + POOL_W
OFF_SC = OFF_SB + SC_W
OFF_SH = OFF_SC + SC_W
OFF_Q = OFF_SH + SC_W
OFF_K = OFF_Q + ATT_W
OFF_V = OFF_K + KV_W
OFF_G = OFF_V + KV_W

LANES = 128
SUBLANES = 8
BF16_ROWS = 16
VMEM_LIMIT_BYTES = 56 * 1024 * 1024

HALO_ATT = WINDOW
HALO_CONV = 16
ROW_CHUNK = 64
COL_CHUNK = 512
ADA_ROWS = 8
ADA_COLS = 1024


def _dot(a, b):
    return jnp.dot(a, b, preferred_element_type=F32)


def _dot_nt(a, b):
    return lax.dot_general(a, b, (((1,), (1,)), ((), ())), preferred_element_type=F32)


def _rms(x):
    return x * lax.rsqrt(jnp.mean(x * x, axis=-1, keepdims=True) + EPS)


def _sigmoid(x):
    return 1.0 / (1.0 + jnp.exp(-x))


def _ada_kernel(cond_ref, w_ref, b_ref, o_ref):
    c = cond_ref[...]
    s = (c * _sigmoid(c)).astype(BF16)
    o_ref[0] = _dot(s, w_ref[0].astype(BF16)) + b_ref[0]


def _ada(cond, w_ada, b_ada):
    n_col = w_ada.shape[-1] // ADA_COLS
    return pl.pallas_call(
        _ada_kernel,
        grid=(DEPTH, n_col),
        in_specs=[
            pl.BlockSpec((ADA_ROWS, D_MODEL), lambda l, c: (0, 0)),
            pl.BlockSpec((1, D_MODEL, ADA_COLS), lambda l, c: (l, 0, c)),
            pl.BlockSpec((1, 1, ADA_COLS), lambda l, c: (l, 0, c)),
        ],
        out_specs=pl.BlockSpec((1, ADA_ROWS, ADA_COLS), lambda l, c: (l, 0, c)),
        out_shape=jax.ShapeDtypeStruct((DEPTH, ADA_ROWS, w_ada.shape[-1]), F32),
        name="ada",
    )(cond, w_ada, b_ada.reshape(DEPTH, 1, -1))


def _mixer_kernel(*refs, names, T, L, halo):
    r = dict(zip(names, refs))
    H = HALO_ATT if halo else 0
    E = T + 2 * H
    TC = T + 2 * HALO_CONV
    BQ = WINDOW if halo else T
    pos0 = pl.program_id(1) * T

    mod = r["mod"][0]
    sh1, sc1, gt1 = mod[0:1], mod[1:2], mod[2:3]
    g_pre = r["g_pre"][...]
    he = r["he"]
    w_in = r["w_in"]

    def norm_mod(x):
        return ((_rms(x) * g_pre) * (1.0 + sc1) + sh1).astype(BF16)

    if halo:
        he[0:H] = norm_mod(r["xp"][0])
        he[H + T:E] = norm_mod(r["xn"][0])
    for i in range(0, T, 256):
        he[H + i:H + i + 256] = norm_mod(r["xm"][0, i:i + 256])

    def he_main():
        return he[H:H + T]

    ua, up, uc = r["ua"], r["up"], r["uc"]
    if halo:
        c_lo, n_c, d_lo = H - HALO_CONV, TC, 0
        rp = pos0 - HALO_CONV + lax.broadcasted_iota(jnp.int32, (TC, 1), 0)
        in_seq = (rp >= 0) & (rp < L)
    else:
        c_lo, n_c, d_lo = 0, T, HALO_CONV
        zeros = jnp.zeros((HALO_CONV, CONV_W), F32)
        for buf in (ua, up, uc):
            buf[0:HALO_CONV] = zeros
            buf[HALO_CONV + T:TC] = zeros

    def seq_mask(v):
        return jnp.where(in_seq, v, 0.0) if halo else v

    za = _dot(he[c_lo:c_lo + n_c], w_in[:, OFF_AVAL:OFF_AVAL + 2 * CONV_W])
    ua[d_lo:d_lo + n_c] = seq_mask(za[:, :CONV_W] * _sigmoid(za[:, CONV_W:]))
    zp = _dot(he[c_lo:c_lo + n_c], w_in[:, OFF_POOL:OFF_POOL + POOL_W])
    up[d_lo:d_lo + n_c] = seq_mask(zp)
    zs = _dot(he[c_lo:c_lo + n_c], w_in[:, OFF_SC:OFF_SC + 2 * SC_W])
    uc[d_lo:d_lo + n_c] = seq_mask(zs[:, :SC_W] * zs[:, SC_W:])

    merged = r["merged"]

    def gate_merge(branch, br_fn):
        for c in range(0, D_MODEL, COL_CHUNK):
            g_off = OFF_G + branch * D_MODEL + c
            g = _sigmoid(_dot(he_main(), w_in[:, g_off:g_off + COL_CHUNK]))
            val = g * br_fn(c)
            if branch == 0:
                merged[:, c:c + COL_CHUNK] = val
            else:
                merged[:, c:c + COL_CHUNK] += val

    act_a, act_p, act_b, act_c = r["act_a"], r["act_p"], r["act_b"], r["act_c"]
    M0 = HALO_CONV

    wca = r["w_conv_a"][...]
    b_conv, ln_g, ln_b = r["b_conv_a"][...], r["ln_g_a"][...], r["ln_b_a"][...]
    for t0 in range(0, T, ROW_CHUNK):
        acc = jnp.zeros((ROW_CHUNK, CONV_W), F32)
        for k in range(CONV_K):
            lo = M0 + t0 + k - CONV_K // 2
            acc = acc + wca[k:k + 1] * ua[lo:lo + ROW_CHUNK]
        acc = acc + b_conv
        mu = jnp.mean(acc, axis=-1, keepdims=True)
        cen = acc - mu
        var = jnp.mean(cen * cen, axis=-1, keepdims=True)
        y = (cen * lax.rsqrt(var + EPS)) * ln_g + ln_b
        act_a[t0:t0 + ROW_CHUNK] = (y * _sigmoid(y)).astype(BF16)
    gate_merge(0, lambda c: _dot(act_a[...], r["w_a_out"][:, c:c + COL_CHUNK]))

    lane = lax.broadcasted_iota(jnp.int32, (ROW_CHUNK, LANES), 1)
    low_half = lane < POOL_G
    for t0 in range(0, T, ROW_CHUNK):
        pos = pos0 + t0 + lax.broadcasted_iota(jnp.int32, (ROW_CHUNK, 1), 0)

        def inv_cnt(w):
            cnt = jnp.minimum(pos + w // 2, L) - jnp.maximum(pos - w // 2, 0)
            return 1.0 / cnt.astype(F32)

        def win(lt, off):
            lo = M0 + t0 + off
            return up[lo:lo + ROW_CHUNK, lt * LANES:(lt + 1) * LANES]

        for lt in range(POOL_W // LANES):
            w_small, w_big = POOL_SIZES[2 * lt], POOL_SIZES[2 * lt + 1]
            tok = win(lt, 0)
            s = win(lt, -1) + tok
            have = 2
            sums = {2: s}
            while have < w_big:
                nxt = 2 * have
                for o in range(have // 2, nxt // 2):
                    s = s + win(lt, -o - 1) + win(lt, o)
                have = nxt
                sums[have] = s
            pooled = jnp.where(low_half, sums[w_small] * inv_cnt(w_small),
                               sums[w_big] * inv_cnt(w_big)) - tok
            act_p[t0:t0 + ROW_CHUNK, lt * LANES:(lt + 1) * LANES] = pooled.astype(BF16)
    ub = _dot(act_p[...], r["w_poolbd"][...]) * r["pool_scale"][...]
    act_b[...] = ub.astype(BF16)
    gate_merge(1, lambda c: _dot(act_b[...], r["w_b_out"][:, c:c + COL_CHUNK]))

    wsc = r["w_sc"][...]
    s_b = _dot(he_main(), w_in[:, OFF_SB:OFF_SB + SC_W])
    for t0 in range(0, T, ROW_CHUNK):
        acc = jnp.zeros((ROW_CHUNK, SC_W), F32)
        for k in range(SC_K):
            lo = M0 + t0 + k - SC_K // 2
            acc = acc + wsc[k:k + 1] * uc[lo:lo + ROW_CHUNK]
        act_c[t0:t0 + ROW_CHUNK] = (s_b[t0:t0 + ROW_CHUNK] * acc).astype(BF16)
    gate_merge(2, lambda c: _dot(act_c[...], r["w_c_out"][:, c:c + COL_CHUNK]))

    lane_e = lax.broadcasted_iota(jnp.int32, (E, LANES), 1)

    def rope(x, tab):
        n = x.shape[0]
        ln = lax.broadcasted_iota(jnp.int32, (n, LANES), 1)
        first = (ln % (HEAD_DIM // 2)) < (HEAD_DIM // 4)
        quarter = HEAD_DIM // 4
        partner = jnp.where(first, pltpu.roll(x, LANES - quarter, 1), pltpu.roll(x, quarter, 1))
        return x * tab[:, :LANES] + partner * tab[:, LANES:]

    def dup_heads(x):
        ln = lax.broadcasted_iota(jnp.int32, x.shape, 1)
        sw = pltpu.roll(x, HEAD_DIM, 1)
        lo = ln < HEAD_DIM
        return jnp.where(lo, x, sw), jnp.where(lo, sw, x)

    kv = _dot(he[...], w_in[:, OFF_K:OFF_K + 2 * KV_W])
    k, v = kv[:, :KV_W], kv[:, KV_W:]
    if halo:
        tab_e = jnp.concatenate([r["rtp"][...], r["rtm"][...], r["rtn"][...]], axis=0)
        k = rope(k, tab_e)
    else:
        r["k_out"][0] = k
        r["v_out"][0] = v
    kd, vd = r["kd"], r["vd"]
    for g, (kk, vv) in enumerate(zip(dup_heads(k), dup_heads(v))):
        kd[g] = kk.astype(BF16)
        vd[g] = vv.astype(BF16)
    if halo:
        for g, (kk, vv) in enumerate(zip(dup_heads(r["ck"][0]), dup_heads(r["cv"][0]))):
            r["kcd"][g] = kk.astype(BF16)
            r["vcd"][g] = vv.astype(BF16)

    q_s = r["q"]
    scale = HEAD_DIM ** -0.5
    for p in range(ATT_W // LANES):
        qp = _dot(he_main(), w_in[:, OFF_Q + p * LANES:OFF_Q + (p + 1) * LANES])
        if halo:
            qp = rope(qp, r["rtm"][...])
        q_s[:, p * LANES:(p + 1) * LANES] = (qp * scale).astype(BF16)

    sink = r["sink"]
    o_s = r["o"]
    lane_q = lax.broadcasted_iota(jnp.int32, (BQ, LANES), 1) < HEAD_DIM
    n_loc = 3 * WINDOW if halo else T
    for j in range(T // BQ):
        rows = slice(j * BQ, (j + 1) * BQ)
        if halo:
            col = lax.broadcasted_iota(jnp.int32, (BQ, n_loc), 1)
            row = lax.broadcasted_iota(jnp.int32, (BQ, n_loc), 0)
            qpos0 = pos0 + j * BQ
            ok = ((col >= jnp.maximum(row, WINDOW - qpos0)) & (col <= row + 2 * WINDOW)
                  & (col < L - qpos0 + WINDOW))
        for g in range(N_KV):
            q0 = q_s[rows, (2 * g) * LANES:(2 * g + 1) * LANES]
            q1 = q_s[rows, (2 * g + 1) * LANES:(2 * g + 2) * LANES]
            zero = jnp.zeros_like(q0)
            qz = jnp.concatenate([jnp.where(lane_q, q0, zero), jnp.where(lane_q, q1, zero),
                                  jnp.where(lane_q, zero, q0), jnp.where(lane_q, zero, q1)], axis=0)
            heads = (4 * g, 4 * g + 2, 4 * g + 1, 4 * g + 3)
            k_loc = kd[g, j * BQ:j * BQ + n_loc] if halo else kd[g]
            v_loc = vd[g, j * BQ:j * BQ + n_loc] if halo else vd[g]
            s_loc = _dot_nt(qz, k_loc)
            if halo:
                s_ctx = _dot_nt(qz, r["kcd"][g])
            p_loc, p_ctx, inv_l = [], [], []
            for b, h in enumerate(heads):
                rb = slice(b * BQ, (b + 1) * BQ)
                sl = s_loc[rb]
                if halo:
                    sl = jnp.where(ok, sl, NEG)
                sk = sink[h]
                m = jnp.maximum(jnp.max(sl, axis=-1, keepdims=True), sk)
                if halo:
                    sc = s_ctx[rb]
                    m = jnp.maximum(m, jnp.max(sc, axis=-1, keepdims=True))
                pl_ = jnp.exp(sl - m)
                den = jnp.sum(pl_, axis=-1, keepdims=True) + jnp.exp(sk - m)
                p_loc.append(pl_.astype(BF16))
                if halo:
                    pc = jnp.exp(sc - m)
                    den = den + jnp.sum(pc, axis=-1, keepdims=True)
                    p_ctx.append(pc.astype(BF16))
                inv_l.append(1.0 / den)
            o = _dot(jnp.concatenate(p_loc, axis=0), v_loc)
            if halo:
                o = o + _dot(jnp.concatenate(p_ctx, axis=0), r["vcd"][g])
            ob = [o[b * BQ:(b + 1) * BQ] * inv_l[b] for b in range(GROUP)]
            o_s[rows, (2 * g) * LANES:(2 * g + 1) * LANES] = jnp.where(lane_q, ob[0], ob[2]).astype(BF16)
            o_s[rows, (2 * g + 1) * LANES:(2 * g + 2) * LANES] = jnp.where(lane_q, ob[1], ob[3]).astype(BF16)
    gate_merge(3, lambda c: _dot(o_s[...], r["w_d_out"][:, c:c + COL_CHUNK]))

    g_post = r["g_post"][...]
    for i in range(0, T, 256):
        m = _dot(merged[i:i + 256].astype(BF16), r["w_o"][...])
        r["x1"][0, i:i + 256] = r["xm"][0, i:i + 256] + gt1 * (_rms(m) * g_post)


def _const_spec(shape):
    nd = len(shape)
    return pl.BlockSpec(shape, lambda b, j: (0,) * nd, pipeline_mode=pl.Buffered(1))


def _mixer(x, mod, mod_row, lw, *, T, ctx_k=None, ctx_v=None, rope_tab=None):
    B, L, _ = x.shape
    halo = ctx_k is not None
    nt = L // T
    H = HALO_ATT if halo else 0
    E = T + 2 * H
    TC = T + 2 * HALO_CONV
    hb = T // HALO_ATT
    n_hb = L // HALO_ATT

    names, args, specs = [], [], []

    def add(name, arr, spec):
        names.append(name)
        args.append(arr)
        specs.append(spec)

    add("xm", x, pl.BlockSpec((1, T, D_MODEL), lambda b, j: (b, j, 0)))
    if halo:
        add("xp", x, pl.BlockSpec((1, HALO_ATT, D_MODEL), lambda b, j: (b, jnp.maximum(j * hb - 1, 0), 0)))
        add("xn", x, pl.BlockSpec((1, HALO_ATT, D_MODEL),
                                  lambda b, j: (b, jnp.minimum((j + 1) * hb, n_hb - 1), 0)))
    add("mod", mod, pl.BlockSpec((1, 3, D_MODEL), mod_row))
    for nm in ("g_pre", "g_post", "w_in", "w_conv_a", "b_conv_a", "ln_g_a", "ln_b_a", "w_a_out", "w_poolbd",
               "pool_scale", "w_b_out", "w_sc", "w_c_out"):
        add(nm, lw[nm], _const_spec(lw[nm].shape))
    add("sink", lw["sink"], pl.BlockSpec(memory_space=pltpu.SMEM))
    for nm in ("w_d_out", "w_o"):
        add(nm, lw[nm], _const_spec(lw[nm].shape))
    if halo:
        add("ck", ctx_k, pl.BlockSpec((1,) + ctx_k.shape[1:], lambda b, j: (b, 0, 0)))
        add("cv", ctx_v, pl.BlockSpec((1,) + ctx_v.shape[1:], lambda b, j: (b, 0, 0)))
        add("rtm", rope_tab, pl.BlockSpec((T, 2 * LANES), lambda b, j: (j, 0)))
        add("rtp", rope_tab, pl.BlockSpec((HALO_ATT, 2 * LANES), lambda b, j: (jnp.maximum(j * hb - 1, 0), 0)))
        add("rtn", rope_tab, pl.BlockSpec((HALO_ATT, 2 * LANES),
                                          lambda b, j: (jnp.minimum((j + 1) * hb, n_hb - 1), 0)))

    out_names = ["x1"]
    out_shape = [jax.ShapeDtypeStruct((B, L, D_MODEL), F32)]
    out_specs = [pl.BlockSpec((1, T, D_MODEL), lambda b, j: (b, j, 0))]
    if not halo:
        for nm in ("k_out", "v_out"):
            out_names.append(nm)
            out_shape.append(jax.ShapeDtypeStruct((B, L, KV_W), F32))
            out_specs.append(pl.BlockSpec((1, T, KV_W), lambda b, j: (b, j, 0)))

    scratch = [("he", pltpu.VMEM((E, D_MODEL), BF16)),
               ("ua", pltpu.VMEM((TC, CONV_W), F32)),
               ("up", pltpu.VMEM((TC, POOL_W), F32)),
               ("uc", pltpu.VMEM((TC, SC_W), F32)),
               ("act_a", pltpu.VMEM((T, CONV_W), BF16)),
               ("act_p", pltpu.VMEM((T, POOL_W), BF16)),
               ("act_b", pltpu.VMEM((T, POOL_W), BF16)),
               ("act_c", pltpu.VMEM((T, SC_W), BF16)),
               ("q", pltpu.VMEM((T, ATT_W), BF16)),
               ("kd", pltpu.VMEM((N_KV, E, LANES), BF16)),
               ("vd", pltpu.VMEM((N_KV, E, LANES), BF16)),
               ("o", pltpu.VMEM((T, ATT_W), BF16)),
               ("merged", pltpu.VMEM((T, D_MODEL), F32))]
    if halo:
        n_ctx = ctx_k.shape[1]
        scratch += [("kcd", pltpu.VMEM((N_KV, n_ctx, LANES), BF16)),
                    ("vcd", pltpu.VMEM((N_KV, n_ctx, LANES), BF16))]

    all_names = tuple(names + out_names + [s[0] for s in scratch])
    kern = functools.partial(_mixer_kernel, names=all_names, T=T, L=L, halo=halo)
    return pl.pallas_call(
        kern,
        grid=(B, nt),
        in_specs=specs,
        out_specs=out_specs,
        out_shape=out_shape,
        scratch_shapes=[s[1] for s in scratch],
        compiler_params=pltpu.CompilerParams(dimension_semantics=("arbitrary", "arbitrary"),
                                             vmem_limit_bytes=VMEM_LIMIT_BYTES),
        name="mixer_latent" if halo else "mixer_context",
    )(*args)


def _ffn_kernel(x_ref, mod_ref, g_pre_ref, g_post_ref, w1_ref, w2_ref, o_ref, h_ref, hid_ref):
    x = x_ref[0]
    mod = mod_ref[0]
    sh2, sc2, gt2 = mod[0:1], mod[1:2], mod[2:3]
    h_ref[...] = ((_rms(x) * g_pre_ref[...]) * (1.0 + sc2) + sh2).astype(BF16)
    for c in range(0, D_FF, COL_CHUNK):
        a = _dot(h_ref[...], w1_ref[:, c:c + COL_CHUNK])
        hid_ref[:, c:c + COL_CHUNK] = jnp.square(jnp.maximum(a, 0.0)).astype(BF16)
    f = _dot(hid_ref[...], w2_ref[...])
    o_ref[0] = x + gt2 * (_rms(f) * g_post_ref[...])


def _ffn(x, mod, mod_row, lw, *, T):
    B, L, _ = x.shape
    return pl.pallas_call(
        _ffn_kernel,
        grid=(B, L // T),
        in_specs=[
            pl.BlockSpec((1, T, D_MODEL), lambda b, j: (b, j, 0)),
            pl.BlockSpec((1, 3, D_MODEL), mod_row),
            _const_spec(lw["g_pre_ffn"].shape),
            _const_spec(lw["g_post_ffn"].shape),
            _const_spec(lw["w_ff1"].shape),
            _const_spec(lw["w_ff2"].shape),
        ],
        out_specs=pl.BlockSpec((1, T, D_MODEL), lambda b, j: (b, j, 0)),
        out_shape=jax.ShapeDtypeStruct(x.shape, F32),
        scratch_shapes=[pltpu.VMEM((T, D_MODEL), BF16), pltpu.VMEM((T, D_FF), BF16)],
        compiler_params=pltpu.CompilerParams(dimension_semantics=("arbitrary", "arbitrary"),
                                             vmem_limit_bytes=VMEM_LIMIT_BYTES),
        name="ffn",
    )(x, mod, lw["g_pre_ffn"], lw["g_post_ffn"], lw["w_ff1"], lw["w_ff2"])


def _rope_table(n):
    half = HEAD_DIM // 4
    freq = ROPE_BASE ** (-jnp.arange(half, dtype=F32) / half)
    pos = jnp.arange(n)
    ang_r = (pos // GRID_W).astype(F32)[:, None] * freq[None, :]
    ang_c = (pos % GRID_W).astype(F32)[:, None] * freq[None, :]
    cos = jnp.concatenate([jnp.cos(ang_r)] * 2 + [jnp.cos(ang_c)] * 2, axis=-1)
    sin = jnp.concatenate([-jnp.sin(ang_r), jnp.sin(ang_r), -jnp.sin(ang_c), jnp.sin(ang_c)], axis=-1)
    return jnp.concatenate([cos, cos, sin, sin], axis=-1)


def _block_diag(w_pool):
    n = w_pool.shape[0]
    rows = [jnp.concatenate([w_pool[g] if h == g else jnp.zeros_like(w_pool[g]) for h in range(n)], axis=1)
            for g in range(n)]
    return jnp.concatenate(rows, axis=0)


def kernel(x_prompt, x_sample, cache_k, cache_v, c, c_ctx, w_ada, b_ada, g_pre_mix, g_post_mix, g_pre_ffn,
           g_post_ffn, w_in, w_conv_a, b_conv_a, ln_g_a, ln_b_a, w_a_out, w_pool, pool_scale, w_b_out, w_sc,
           w_c_out, sink, w_d_out, w_o, w_ff1, w_ff2):
    n_dec = c.shape[0]
    ctx_row = n_dec
    cond = jnp.concatenate([c, c_ctx[None, :], jnp.zeros((ADA_ROWS - n_dec - 1, D_MODEL), F32)], axis=0)
    mod = _ada(cond, w_ada, b_ada).reshape(DEPTH, ADA_ROWS, 6, D_MODEL)

    def row2(a):
        return a.reshape(1, -1)

    def layer_weights(l):
        return {
            "g_pre": row2(g_pre_mix[l]), "g_post": row2(g_post_mix[l]),
            "g_pre_ffn": row2(g_pre_ffn[l]), "g_post_ffn": row2(g_post_ffn[l]),
            "w_in": w_in[l].astype(BF16),
            "w_conv_a": w_conv_a[l], "b_conv_a": row2(b_conv_a[l]),
            "ln_g_a": row2(ln_g_a[l]), "ln_b_a": row2(ln_b_a[l]),
            "w_a_out": w_a_out[l].astype(BF16),
            "w_poolbd": _block_diag(w_pool[l]).astype(BF16), "pool_scale": row2(pool_scale[l]),
            "w_b_out": w_b_out[l].astype(BF16),
            "w_sc": w_sc[l], "w_c_out": w_c_out[l].astype(BF16),
            "sink": sink[l],
            "w_d_out": w_d_out[l].astype(BF16), "w_o": w_o[l].astype(BF16),
            "w_ff1": w_ff1[l].astype(BF16), "w_ff2": w_ff2[l].astype(BF16),
        }

    lws = [layer_weights(l) for l in range(DEPTH)]
    rope_tab = _rope_table(x_sample.shape[1])
    n_b, n_s = x_prompt.shape[:2]
    ck = cache_k.reshape(cache_k.shape[:3] + (KV_W,))
    cv = cache_v.reshape(cache_v.shape[:3] + (KV_W,))

    xp, xs = x_prompt, x_sample
    ks, vs = [], []
    for l in range(DEPTH):
        mod_mix, mod_ffn = mod[l, :, 0:3], mod[l, :, 3:6]
        xp, k, v = _mixer(xp, mod_mix, lambda b, j: (ctx_row, 0, 0), lws[l], T=n_s)
        xp = _ffn(xp.reshape(n_b // 2, 2 * n_s, D_MODEL), mod_ffn, lambda b, j: (ctx_row, 0, 0), lws[l],
                  T=2 * n_s).reshape(n_b, n_s, D_MODEL)
        ks.append(k.reshape(n_b, n_s, N_KV, HEAD_DIM))
        vs.append(v.reshape(n_b, n_s, N_KV, HEAD_DIM))
        xs = _mixer(xs, mod_mix, lambda b, j: (b, 0, 0), lws[l], T=512, ctx_k=ck[:, l], ctx_v=cv[:, l],
                    rope_tab=rope_tab)[0]
        xs = _ffn(xs, mod_ffn, lambda b, j: (b, 0, 0), lws[l], T=512)
    return (xp, xs, jnp.stack(ks, axis=1), jnp.stack(vs, axis=1))
```

```python
import functools

import jax
import jax.numpy as jnp
import numpy as np
from jax import lax
from jax.experimental import pallas as pl
from jax.experimental.pallas import tpu as pltpu

F32 = jnp.float32
BF16 = jnp.bfloat16

D_MODEL = 1024
DEPTH = 2
GRID_W = 64
N_HEADS = 8
N_KV = 2
HEAD_DIM = 64
GROUP = N_HEADS // N_KV
ATT_W = N_HEADS * HEAD_DIM
KV_W = N_KV * HEAD_DIM
WINDOW = 128
ROPE_BASE = 10000.0
CONV_W = D_MODEL // 4
CONV_K = 31
POOL_W = D_MODEL // 4
POOL_SIZES = (2, 4, 8, 16)
POOL_G = POOL_W // len(POOL_SIZES)
SC_W = D_MODEL // 4
SC_K = 3
N_BRANCH = 4
D_FF = 4 * D_MODEL
EPS = 1e-6
NEG = -1e30
IN_W = 2 * CONV_W + POOL_W + 3 * SC_W + ATT_W + 2 * KV_W + N_BRANCH * D_MODEL

OFF_AVAL = 0
OFF_AGATE = OFF_AVAL + CONV_W
OFF_POOL = OFF_AGATE + CONV_W
OFF_SB = OFF_POOL + POOL_W
OFF_SC = OFF_SB + SC_W
OFF_SH = OFF_SC + SC_W
OFF_Q = OFF_SH + SC_W
OFF_K = OFF_Q + ATT_W
OFF_V = OFF_K + KV_W
OFF_G = OFF_V + KV_W

LANES = 128
SUBLANES = 8
BF16_ROWS = 16
VMEM_LIMIT_BYTES = 56 * 1024 * 1024

HALO_ATT = WINDOW
HALO_CONV = 16
ROW_CHUNK = 64
COL_CHUNK = 512
ADA_ROWS = 8
ADA_COLS = 1024


def _dot(a, b):
    return jnp.dot(a, b, preferred_element_type=F32)


def _dot_nt(a, b):
    return lax.dot_general(a, b, (((1,), (1,)), ((), ())), preferred_element_type=F32)


def _rms(x):
    return x * lax.rsqrt(jnp.mean(x * x, axis=-1, keepdims=True) + EPS)


def _sigmoid(x):
    return 0.5 * jnp.tanh(0.5 * x) + 0.5


def _ada_kernel(cond_ref, w_ref, b_ref, o_ref):
    c = cond_ref[...]
    s = (c * _sigmoid(c)).astype(BF16)
    o_ref[0] = _dot(s, w_ref[0].astype(BF16)) + b_ref[0]


def _ada(cond, w_ada, b_ada):
    n_col = w_ada.shape[-1] // ADA_COLS
    return pl.pallas_call(
        _ada_kernel,
        grid=(DEPTH, n_col),
        in_specs=[
            pl.BlockSpec((ADA_ROWS, D_MODEL), lambda l, c: (0, 0)),
            pl.BlockSpec((1, D_MODEL, ADA_COLS), lambda l, c: (l, 0, c)),
            pl.BlockSpec((1, 1, ADA_COLS), lambda l, c: (l, 0, c)),
        ],
        out_specs=pl.BlockSpec((1, ADA_ROWS, ADA_COLS), lambda l, c: (l, 0, c)),
        out_shape=jax.ShapeDtypeStruct((DEPTH, ADA_ROWS, w_ada.shape[-1]), F32),
        name="ada",
    )(cond, w_ada, b_ada.reshape(DEPTH, 1, -1))


def _mixer_kernel(*refs, names, T, L, halo):
    r = dict(zip(names, refs))
    H = HALO_ATT if halo else 0
    E = T + 2 * H
    TC = T + 2 * HALO_CONV
    BQ = WINDOW if halo else T
    pos0 = pl.program_id(1) * T

    mod = r["mod"][0]
    sh1, sc1, gt1 = mod[0:1], mod[1:2], mod[2:3]
    g_pre = r["g_pre"][...]
    he = r["he"]
    w_in = r["w_in"]

    def norm_mod(x):
        return ((_rms(x) * g_pre) * (1.0 + sc1) + sh1).astype(BF16)

    if halo:
        he[0:H] = norm_mod(r["xp"][0])
        he[H + T:E] = norm_mod(r["xn"][0])
    for i in range(0, T, 256):
        he[H + i:H + i + 256] = norm_mod(r["xm"][0, i:i + 256])

    def he_main():
        return he[H:H + T]

    ua, up, uc = r["ua"], r["up"], r["uc"]
    if halo:
        c_lo, n_c, d_lo = H - HALO_CONV, TC, 0
        rp = pos0 - HALO_CONV + lax.broadcasted_iota(jnp.int32, (TC, 1), 0)
        in_seq = (rp >= 0) & (rp < L)
    else:
        c_lo, n_c, d_lo = 0, T, HALO_CONV
        zeros = jnp.zeros((HALO_CONV, CONV_W), F32)
        for buf in (ua, up, uc):
            buf[0:HALO_CONV] = zeros
            buf[HALO_CONV + T:TC] = zeros
    ua[TC:TC + SUBLANES] = jnp.zeros((SUBLANES, CONV_W), F32)

    def seq_mask(v):
        return jnp.where(in_seq, v, 0.0) if halo else v

    za = _dot(he[c_lo:c_lo + n_c], w_in[:, OFF_AVAL:OFF_AVAL + 2 * CONV_W])
    ua[d_lo:d_lo + n_c] = seq_mask(za[:, :CONV_W] * _sigmoid(za[:, CONV_W:]))
    zp = _dot(he[c_lo:c_lo + n_c], w_in[:, OFF_POOL:OFF_POOL + POOL_W])
    up[d_lo:d_lo + n_c] = seq_mask(zp)
    zs = _dot(he[c_lo:c_lo + n_c], w_in[:, OFF_SC:OFF_SC + 2 * SC_W])
    uc[d_lo:d_lo + n_c] = seq_mask(zs[:, :SC_W] * zs[:, SC_W:])

    merged = r["merged"]

    def gate_merge(branch, br_fn):
        for c in range(0, D_MODEL, COL_CHUNK):
            g_off = OFF_G + branch * D_MODEL + c
            g = _sigmoid(_dot(he_main(), w_in[:, g_off:g_off + COL_CHUNK]))
            val = g * br_fn(c)
            if branch == 0:
                merged[:, c:c + COL_CHUNK] = val
            else:
                merged[:, c:c + COL_CHUNK] += val

    act_a, act_p, act_b, act_c = r["act_a"], r["act_p"], r["act_b"], r["act_c"]
    M0 = HALO_CONV

    wca = r["w_conv_a"][...]
    b_conv, ln_g, ln_b = r["b_conv_a"][...], r["ln_g_a"][...], r["ln_b_a"][...]
    uas = r["uas"]
    for s in range(1, SUBLANES):
        uas[s - 1] = ua[s:s + TC]
    for t0 in range(0, T, ROW_CHUNK):
        acc = jnp.zeros((ROW_CHUNK, CONV_W), F32)
        for k in range(CONV_K):
            lo = M0 + t0 + k - CONV_K // 2
            s, base = lo % SUBLANES, lo - lo % SUBLANES
            win = ua[base:base + ROW_CHUNK] if s == 0 else uas[s - 1, base:base + ROW_CHUNK]
            acc = acc + wca[k:k + 1] * win
        acc = acc + b_conv
        mu = jnp.mean(acc, axis=-1, keepdims=True)
        cen = acc - mu
        var = jnp.mean(cen * cen, axis=-1, keepdims=True)
        y = (cen * lax.rsqrt(var + EPS)) * ln_g + ln_b
        act_a[t0:t0 + ROW_CHUNK] = (y * _sigmoid(y)).astype(BF16)
    gate_merge(0, lambda c: _dot(act_a[...], r["w_a_out"][:, c:c + COL_CHUNK]))

    lane = lax.broadcasted_iota(jnp.int32, (ROW_CHUNK, LANES), 1)
    low_half = lane < POOL_G
    for t0 in range(0, T, ROW_CHUNK):
        pos = pos0 + t0 + lax.broadcasted_iota(jnp.int32, (ROW_CHUNK, 1), 0)

        def inv_cnt(w):
            cnt = jnp.minimum(pos + w // 2, L) - jnp.maximum(pos - w // 2, 0)
            return 1.0 / cnt.astype(F32)

        def win(lt, off):
            lo = M0 + t0 + off
            return up[lo:lo + ROW_CHUNK, lt * LANES:(lt + 1) * LANES]

        for lt in range(POOL_W // LANES):
            w_small, w_big = POOL_SIZES[2 * lt], POOL_SIZES[2 * lt + 1]
            tok = win(lt, 0)
            s = win(lt, -1) + tok
            have = 2
            sums = {2: s}
            while have < w_big:
                nxt = 2 * have
                for o in range(have // 2, nxt // 2):
                    s = s + win(lt, -o - 1) + win(lt, o)
                have = nxt
                sums[have] = s
            pooled = jnp.where(low_half, sums[w_small] * inv_cnt(w_small),
                               sums[w_big] * inv_cnt(w_big)) - tok
            act_p[t0:t0 + ROW_CHUNK, lt * LANES:(lt + 1) * LANES] = pooled.astype(BF16)
    ub = _dot(act_p[...], r["w_poolbd"][...]) * r["pool_scale"][...]
    act_b[...] = ub.astype(BF16)
    gate_merge(1, lambda c: _dot(act_b[...], r["w_b_out"][:, c:c + COL_CHUNK]))

    wsc = r["w_sc"][...]
    s_b = _dot(he_main(), w_in[:, OFF_SB:OFF_SB + SC_W])
    for t0 in range(0, T, ROW_CHUNK):
        acc = jnp.zeros((ROW_CHUNK, SC_W), F32)
        for k in range(SC_K):
            lo = M0 + t0 + k - SC_K // 2
            acc = acc + wsc[k:k + 1] * uc[lo:lo + ROW_CHUNK]
        act_c[t0:t0 + ROW_CHUNK] = (s_b[t0:t0 + ROW_CHUNK] * acc).astype(BF16)
    gate_merge(2, lambda c: _dot(act_c[...], r["w_c_out"][:, c:c + COL_CHUNK]))

    def rope(x, tab):
        n = x.shape[0]
        ln = lax.broadcasted_iota(jnp.int32, (n, LANES), 1)
        first = (ln % (HEAD_DIM // 2)) < (HEAD_DIM // 4)
        quarter = HEAD_DIM // 4
        partner = jnp.where(first, pltpu.roll(x, LANES - quarter, 1), pltpu.roll(x, quarter, 1))
        return x * tab[:, :LANES] + partner * tab[:, LANES:]

    def dup_heads(x):
        ln = lax.broadcasted_iota(jnp.int32, x.shape, 1)
        sw = pltpu.roll(x, HEAD_DIM, 1)
        lo = ln < HEAD_DIM
        return jnp.where(lo, x, sw), jnp.where(lo, sw, x)

    kv = _dot(he[...], w_in[:, OFF_K:OFF_K + 2 * KV_W])
    k, v = kv[:, :KV_W], kv[:, KV_W:]
    if halo:
        tab_e = jnp.concatenate([r["rtp"][...], r["rtm"][...], r["rtn"][...]], axis=0)
        k = rope(k, tab_e)
    else:
        r["k_out"][0] = k
        r["v_out"][0] = v
    kd, vd = r["kd"], r["vd"]
    for g, (kk, vv) in enumerate(zip(dup_heads(k), dup_heads(v))):
        kd[g] = kk.astype(BF16)
        vd[g] = vv.astype(BF16)
    if halo:
        for g, (kk, vv) in enumerate(zip(dup_heads(r["ck"][0]), dup_heads(r["cv"][0]))):
            r["kcd"][g] = kk.astype(BF16)
            r["vcd"][g] = vv.astype(BF16)

    q_s = r["q"]
    scale = HEAD_DIM ** -0.5
    for p in range(ATT_W // LANES):
        qp = _dot(he_main(), w_in[:, OFF_Q + p * LANES:OFF_Q + (p + 1) * LANES])
        if halo:
            qp = rope(qp, r["rtm"][...])
        q_s[:, p * LANES:(p + 1) * LANES] = (qp * scale).astype(BF16)

    sink = r["sink"]
    o_s = r["o"]
    lane_q = lax.broadcasted_iota(jnp.int32, (BQ, LANES), 1) < HEAD_DIM
    n_loc = 3 * WINDOW if halo else T
    for j in range(T // BQ):
        rows = slice(j * BQ, (j + 1) * BQ)
        if halo:
            col = lax.broadcasted_iota(jnp.int32, (BQ, WINDOW), 1)
            row = lax.broadcasted_iota(jnp.int32, (BQ, WINDOW), 0)
            qpos0 = pos0 + j * BQ
            ok_prev = col >= row + jnp.where(qpos0 >= WINDOW, 0, WINDOW)
            ok_next = col <= row - jnp.where(qpos0 + 2 * WINDOW <= L, 0, WINDOW)
        for g in range(N_KV):
            q0 = q_s[rows, (2 * g) * LANES:(2 * g + 1) * LANES]
            q1 = q_s[rows, (2 * g + 1) * LANES:(2 * g + 2) * LANES]
            zero = jnp.zeros_like(q0)
            qz = jnp.concatenate([jnp.where(lane_q, q0, zero), jnp.where(lane_q, q1, zero),
                                  jnp.where(lane_q, zero, q0), jnp.where(lane_q, zero, q1)], axis=0)
            heads = (4 * g, 4 * g + 2, 4 * g + 1, 4 * g + 3)
            k_loc = kd[g, j * BQ:j * BQ + n_loc] if halo else kd[g]
            v_loc = vd[g, j * BQ:j * BQ + n_loc] if halo else vd[g]
            s_loc = _dot_nt(qz, k_loc)
            if halo:
                s_ctx = _dot_nt(qz, r["kcd"][g])
            p_loc, p_ctx, inv_l = [], [], []
            for b, h in enumerate(heads):
                rb = slice(b * BQ, (b + 1) * BQ)
                sl = s_loc[rb]
                if halo:
                    sl = jnp.concatenate([jnp.where(ok_prev, sl[:, :WINDOW], NEG), sl[:, WINDOW:2 * WINDOW],
                                          jnp.where(ok_next, sl[:, 2 * WINDOW:], NEG)], axis=1)
                sk = sink[h]
                m = jnp.maximum(jnp.max(sl, axis=-1, keepdims=True), sk)
                if halo:
                    sc = s_ctx[rb]
                    m = jnp.maximum(m, jnp.max(sc, axis=-1, keepdims=True))
                pl_ = jnp.exp(sl - m)
                den = jnp.sum(pl_, axis=-1, keepdims=True) + jnp.exp(sk - m)
                p_loc.append(pl_.astype(BF16))
                if halo:
                    pc = jnp.exp(sc - m)
                    den = den + jnp.sum(pc, axis=-1, keepdims=True)
                    p_ctx.append(pc.astype(BF16))
                inv_l.append(1.0 / den)
            o = _dot(jnp.concatenate(p_loc, axis=0), v_loc)
            if halo:
                o = o + _dot(jnp.concatenate(p_ctx, axis=0), r["vcd"][g])
            ob = [o[b * BQ:(b + 1) * BQ] * inv_l[b] for b in range(GROUP)]
            o_s[rows, (2 * g) * LANES:(2 * g + 1) * LANES] = jnp.where(lane_q, ob[0], ob[2]).astype(BF16)
            o_s[rows, (2 * g + 1) * LANES:(2 * g + 2) * LANES] = jnp.where(lane_q, ob[1], ob[3]).astype(BF16)
    gate_merge(3, lambda c: _dot(o_s[...], r["w_d_out"][:, c:c + COL_CHUNK]))

    g_post = r["g_post"][...]
    for i in range(0, T, 256):
        m = _dot(merged[i:i + 256].astype(BF16), r["w_o"][...])
        r["x1"][0, i:i + 256] = r["xm"][0, i:i + 256] + gt1 * (_rms(m) * g_post)


def _layer_spec(arr, l):
    nd = arr.ndim - 1
    return pl.BlockSpec((None,) + arr.shape[1:], lambda b, j: (l,) + (0,) * nd, pipeline_mode=pl.Buffered(1))


def _mixer(x, mod, mod_row, lw, l, *, T, ctx_k=None, ctx_v=None, rope_tab=None):
    B, L, _ = x.shape
    halo = ctx_k is not None
    nt = L // T
    H = HALO_ATT if halo else 0
    E = T + 2 * H
    TC = T + 2 * HALO_CONV
    hb = T // HALO_ATT
    n_hb = L // HALO_ATT

    names, args, specs = [], [], []

    def add(name, arr, spec):
        names.append(name)
        args.append(arr)
        specs.append(spec)

    add("xm", x, pl.BlockSpec((1, T, D_MODEL), lambda b, j: (b, j, 0)))
    if halo:
        add("xp", x, pl.BlockSpec((1, HALO_ATT, D_MODEL), lambda b, j: (b, jnp.maximum(j * hb - 1, 0), 0)))
        add("xn", x, pl.BlockSpec((1, HALO_ATT, D_MODEL),
                                  lambda b, j: (b, jnp.minimum((j + 1) * hb, n_hb - 1), 0)))
    add("mod", mod, pl.BlockSpec((1, 3, D_MODEL), mod_row))
    for nm in ("g_pre", "g_post", "w_in", "w_conv_a", "b_conv_a", "ln_g_a", "ln_b_a", "w_a_out", "w_poolbd",
               "pool_scale", "w_b_out", "w_sc", "w_c_out"):
        add(nm, lw[nm], _layer_spec(lw[nm], l))
    add("sink", lw["sink"][l], pl.BlockSpec(memory_space=pltpu.SMEM))
    for nm in ("w_d_out", "w_o"):
        add(nm, lw[nm], _layer_spec(lw[nm], l))
    if halo:
        kv_spec = pl.BlockSpec((1, None) + ctx_k.shape[2:], lambda b, j: (b, l, 0, 0))
        add("ck", ctx_k, kv_spec)
        add("cv", ctx_v, kv_spec)
        add("rtm", rope_tab, pl.BlockSpec((T, 2 * LANES), lambda b, j: (j, 0)))
        add("rtp", rope_tab, pl.BlockSpec((HALO_ATT, 2 * LANES), lambda b, j: (jnp.maximum(j * hb - 1, 0), 0)))
        add("rtn", rope_tab, pl.BlockSpec((HALO_ATT, 2 * LANES),
                                          lambda b, j: (jnp.minimum((j + 1) * hb, n_hb - 1), 0)))

    out_names = ["x1"]
    out_shape = [jax.ShapeDtypeStruct((B, L, D_MODEL), F32)]
    out_specs = [pl.BlockSpec((1, T, D_MODEL), lambda b, j: (b, j, 0))]
    if not halo:
        for nm in ("k_out", "v_out"):
            out_names.append(nm)
            out_shape.append(jax.ShapeDtypeStruct((B, L, KV_W), F32))
            out_specs.append(pl.BlockSpec((1, T, KV_W), lambda b, j: (b, j, 0)))

    scratch = [("he", pltpu.VMEM((E, D_MODEL), BF16)),
               ("ua", pltpu.VMEM((TC + SUBLANES, CONV_W), F32)),
               ("uas", pltpu.VMEM((SUBLANES - 1, TC, CONV_W), F32)),
               ("up", pltpu.VMEM((TC, POOL_W), F32)),
               ("uc", pltpu.VMEM((TC, SC_W), F32)),
               ("act_a", pltpu.VMEM((T, CONV_W), BF16)),
               ("act_p", pltpu.VMEM((T, POOL_W), BF16)),
               ("act_b", pltpu.VMEM((T, POOL_W), BF16)),
               ("act_c", pltpu.VMEM((T, SC_W), BF16)),
               ("q", pltpu.VMEM((T, ATT_W), BF16)),
               ("kd", pltpu.VMEM((N_KV, E, LANES), BF16)),
               ("vd", pltpu.VMEM((N_KV, E, LANES), BF16)),
               ("o", pltpu.VMEM((T, ATT_W), BF16)),
               ("merged", pltpu.VMEM((T, D_MODEL), F32))]
    if halo:
        n_ctx = ctx_k.shape[2]
        scratch += [("kcd", pltpu.VMEM((N_KV, n_ctx, LANES), BF16)),
                    ("vcd", pltpu.VMEM((N_KV, n_ctx, LANES), BF16))]

    all_names = tuple(names + out_names + [s[0] for s in scratch])
    kern = functools.partial(_mixer_kernel, names=all_names, T=T, L=L, halo=halo)
    return pl.pallas_call(
        kern,
        grid=(B, nt),
        in_specs=specs,
        out_specs=out_specs,
        out_shape=out_shape,
        scratch_shapes=[s[1] for s in scratch],
        compiler_params=pltpu.CompilerParams(dimension_semantics=("arbitrary", "arbitrary"),
                                             vmem_limit_bytes=VMEM_LIMIT_BYTES),
        name="mixer_latent" if halo else "mixer_context",
    )(*args)


def _ffn_kernel(x_ref, mod_ref, g_pre_ref, g_post_ref, w1_ref, w2_ref, o_ref, h_ref, hid_ref):
    x = x_ref[0]
    mod = mod_ref[0]
    sh2, sc2, gt2 = mod[0:1], mod[1:2], mod[2:3]
    h_ref[...] = ((_rms(x) * g_pre_ref[...]) * (1.0 + sc2) + sh2).astype(BF16)
    for c in range(0, D_FF, COL_CHUNK):
        a = _dot(h_ref[...], w1_ref[:, c:c + COL_CHUNK])
        hid_ref[:, c:c + COL_CHUNK] = jnp.square(jnp.maximum(a, 0.0)).astype(BF16)
    f = _dot(hid_ref[...], w2_ref[...])
    o_ref[0] = x + gt2 * (_rms(f) * g_post_ref[...])


def _ffn(x, mod, mod_row, lw, l, *, T):
    B, L, _ = x.shape
    return pl.pallas_call(
        _ffn_kernel,
        grid=(B, L // T),
        in_specs=[
            pl.BlockSpec((1, T, D_MODEL), lambda b, j: (b, j, 0)),
            pl.BlockSpec((1, 3, D_MODEL), mod_row),
            _layer_spec(lw["g_pre_ffn"], l),
            _layer_spec(lw["g_post_ffn"], l),
            _layer_spec(lw["w_ff1"], l),
            _layer_spec(lw["w_ff2"], l),
        ],
        out_specs=pl.BlockSpec((1, T, D_MODEL), lambda b, j: (b, j, 0)),
        out_shape=jax.ShapeDtypeStruct(x.shape, F32),
        scratch_shapes=[pltpu.VMEM((T, D_MODEL), BF16), pltpu.VMEM((T, D_FF), BF16)],
        compiler_params=pltpu.CompilerParams(dimension_semantics=("arbitrary", "arbitrary"),
                                             vmem_limit_bytes=VMEM_LIMIT_BYTES),
        name="ffn",
    )(x, mod, lw["g_pre_ffn"], lw["g_post_ffn"], lw["w_ff1"], lw["w_ff2"])


def _rope_table(n):
    half = HEAD_DIM // 4
    freq = (np.float32(ROPE_BASE) ** (-np.arange(half, dtype=np.float32) / np.float32(half))).astype(np.float32)
    pos = np.arange(n)
    ang_r = ((pos // GRID_W).astype(np.float32)[:, None] * freq[None, :]).astype(np.float32)
    ang_c = ((pos % GRID_W).astype(np.float32)[:, None] * freq[None, :]).astype(np.float32)
    cos = np.concatenate([np.cos(ang_r)] * 2 + [np.cos(ang_c)] * 2, axis=-1)
    sin = np.concatenate([-np.sin(ang_r), np.sin(ang_r), -np.sin(ang_c), np.sin(ang_c)], axis=-1)
    return jnp.asarray(np.concatenate([cos, cos, sin, sin], axis=-1), dtype=F32)


def _block_diag(w_pool):
    n = w_pool.shape[0]
    rows = [jnp.concatenate([w_pool[g] if h == g else jnp.zeros_like(w_pool[g]) for h in range(n)], axis=1)
            for g in range(n)]
    return jnp.concatenate(rows, axis=0)


def kernel(x_prompt, x_sample, cache_k, cache_v, c, c_ctx, w_ada, b_ada, g_pre_mix, g_post_mix, g_pre_ffn,
           g_post_ffn, w_in, w_conv_a, b_conv_a, ln_g_a, ln_b_a, w_a_out, w_pool, pool_scale, w_b_out, w_sc,
           w_c_out, sink, w_d_out, w_o, w_ff1, w_ff2):
    n_dec = c.shape[0]
    ctx_row = n_dec
    cond = jnp.concatenate([c, c_ctx[None, :], jnp.zeros((ADA_ROWS - n_dec - 1, D_MODEL), F32)], axis=0)
    mod = _ada(cond, w_ada, b_ada).reshape(DEPTH, ADA_ROWS, 6, D_MODEL)

    def rows(a):
        return a.reshape(DEPTH, 1, -1)

    lw = {
        "g_pre": rows(g_pre_mix), "g_post": rows(g_post_mix),
        "g_pre_ffn": rows(g_pre_ffn), "g_post_ffn": rows(g_post_ffn),
        "w_in": w_in.astype(BF16),
        "w_conv_a": w_conv_a, "b_conv_a": rows(b_conv_a), "ln_g_a": rows(ln_g_a), "ln_b_a": rows(ln_b_a),
        "w_a_out": w_a_out.astype(BF16),
        "w_poolbd": jnp.stack([_block_diag(w_pool[l]) for l in range(DEPTH)]).astype(BF16),
        "pool_scale": rows(pool_scale), "w_b_out": w_b_out.astype(BF16),
        "w_sc": w_sc, "w_c_out": w_c_out.astype(BF16), "sink": sink,
        "w_d_out": w_d_out.astype(BF16), "w_o": w_o.astype(BF16),
        "w_ff1": w_ff1.astype(BF16), "w_ff2": w_ff2.astype(BF16),
    }
    rope_tab = _rope_table(x_sample.shape[1])
    n_b, n_s = x_prompt.shape[:2]
    ck = cache_k.reshape(cache_k.shape[:3] + (KV_W,))
    cv = cache_v.reshape(cache_v.shape[:3] + (KV_W,))

    xp, xs = x_prompt, x_sample
    ks, vs = [], []
    for l in range(DEPTH):
        mod_mix, mod_ffn = mod[l, :, 0:3], mod[l, :, 3:6]
        xp, k, v = _mixer(xp, mod_mix, lambda b, j: (ctx_row, 0, 0), lw, l, T=n_s)
        xp = _ffn(xp.reshape(n_b // 2, 2 * n_s, D_MODEL), mod_ffn, lambda b, j: (ctx_row, 0, 0), lw, l,
                  T=2 * n_s).reshape(n_b, n_s, D_MODEL)
        ks.append(k.reshape(n_b, n_s, N_KV, HEAD_DIM))
        vs.append(v.reshape(n_b, n_s, N_KV, HEAD_DIM))
        xs = _mixer(xs, mod_mix, lambda b, j: (b, 0, 0), lw, l, T=512, ctx_k=ck, ctx_v=cv,
                    rope_tab=rope_tab)[0]
        xs = _ffn(xs, mod_ffn, lambda b, j: (b, 0, 0), lw, l, T=512)
    return (xp, xs, jnp.stack(ks, axis=1), jnp.stack(vs, axis=1))
```

```python
import functools

import jax
import jax.numpy as jnp
import numpy as np
from jax import lax
from jax.experimental import pallas as pl
from jax.experimental.pallas import tpu as pltpu

F32 = jnp.float32
BF16 = jnp.bfloat16

D_MODEL = 1024
DEPTH = 2
GRID_W = 64
N_HEADS = 8
N_KV = 2
HEAD_DIM = 64
GROUP = N_HEADS // N_KV
ATT_W = N_HEADS * HEAD_DIM
KV_W = N_KV * HEAD_DIM
WINDOW = 128
ROPE_BASE = 10000.0
CONV_W = D_MODEL // 4
CONV_K = 31
POOL_W = D_MODEL // 4
POOL_SIZES = (2, 4, 8, 16)
POOL_G = POOL_W // len(POOL_SIZES)
SC_W = D_MODEL // 4
SC_K = 3
N_BRANCH = 4
D_FF = 4 * D_MODEL
EPS = 1e-6
NEG = -1e30
IN_W = 2 * CONV_W + POOL_W + 3 * SC_W + ATT_W + 2 * KV_W + N_BRANCH * D_MODEL

OFF_AVAL = 0
OFF_AGATE = OFF_AVAL + CONV_W
OFF_POOL = OFF_AGATE + CONV_W
OFF_SB = OFF_POOL + POOL_W
OFF_SC = OFF_SB + SC_W
OFF_SH = OFF_SC + SC_W
OFF_Q = OFF_SH + SC_W
OFF_K = OFF_Q + ATT_W
OFF_V = OFF_K + KV_W
OFF_G = OFF_V + KV_W

LANES = 128
SUBLANES = 8
BF16_ROWS = 16
VMEM_LIMIT_BYTES = 56 * 1024 * 1024

HALO_ATT = WINDOW
HALO_CONV = 16
ROW_CHUNK = 64
COL_CHUNK = 512
MIX_CHUNK = 256
ADA_ROWS = 8
ADA_COLS = 1024


def _dot(a, b):
    return jnp.dot(a, b, preferred_element_type=F32)


def _dot_nt(a, b):
    return lax.dot_general(a, b, (((1,), (1,)), ((), ())), preferred_element_type=F32)


def _rms(x):
    return x * lax.rsqrt(jnp.mean(x * x, axis=-1, keepdims=True) + EPS)


def _sigmoid(x):
    return 0.5 * jnp.tanh(0.5 * x) + 0.5


def _ada_kernel(cond_ref, w_ref, b_ref, o_ref):
    c = cond_ref[...]
    s = (c * _sigmoid(c)).astype(BF16)
    o_ref[0] = _dot(s, w_ref[0].astype(BF16)) + b_ref[0]


def _ada(cond, w_ada, b_ada):
    n_col = w_ada.shape[-1] // ADA_COLS
    return pl.pallas_call(
        _ada_kernel,
        grid=(DEPTH, n_col),
        in_specs=[
            pl.BlockSpec((ADA_ROWS, D_MODEL), lambda l, c: (0, 0)),
            pl.BlockSpec((1, D_MODEL, ADA_COLS), lambda l, c: (l, 0, c)),
            pl.BlockSpec((1, 1, ADA_COLS), lambda l, c: (l, 0, c)),
        ],
        out_specs=pl.BlockSpec((1, ADA_ROWS, ADA_COLS), lambda l, c: (l, 0, c)),
        out_shape=jax.ShapeDtypeStruct((DEPTH, ADA_ROWS, w_ada.shape[-1]), F32),
        name="ada",
    )(cond, w_ada, b_ada.reshape(DEPTH, 1, -1))


def _mixer_kernel(*refs, names, T, L, halo):
    r = dict(zip(names, refs))
    H = HALO_ATT if halo else 0
    E = T + 2 * H
    TC = T + 2 * HALO_CONV
    BQ = WINDOW if halo else T
    pos0 = pl.program_id(1) * T

    mod = r["mod"][0]
    sh1, sc1, gt1 = mod[0:1], mod[1:2], mod[2:3]
    g_pre = r["g_pre"][...]
    he = r["he"]
    w_in = r["w_in"]

    def norm_mod(x):
        return ((_rms(x) * g_pre) * (1.0 + sc1) + sh1).astype(BF16)

    if halo:
        he[0:H] = norm_mod(r["xp"][0])
        he[H + T:E] = norm_mod(r["xn"][0])
    for i in range(0, T, 256):
        he[H + i:H + i + 256] = norm_mod(r["xm"][0, i:i + 256])

    def he_main():
        return he[H:H + T]

    ua, up, uc = r["ua"], r["up"], r["uc"]
    if halo:
        c_lo, n_c, d_lo = H - HALO_CONV, TC, 0
        rp = pos0 - HALO_CONV + lax.broadcasted_iota(jnp.int32, (TC, 1), 0)
        in_seq = (rp >= 0) & (rp < L)
    else:
        c_lo, n_c, d_lo = 0, T, HALO_CONV
        zeros = jnp.zeros((HALO_CONV, CONV_W), F32)
        for buf in (ua, up, uc):
            buf[0:HALO_CONV] = zeros
            buf[HALO_CONV + T:TC] = zeros
    ua[TC:TC + SUBLANES] = jnp.zeros((SUBLANES, CONV_W), F32)

    def seq_mask(v):
        return jnp.where(in_seq, v, 0.0) if halo else v

    za = _dot(he[c_lo:c_lo + n_c], w_in[:, OFF_AVAL:OFF_AVAL + 2 * CONV_W])
    ua[d_lo:d_lo + n_c] = seq_mask(za[:, :CONV_W] * _sigmoid(za[:, CONV_W:]))
    zp = _dot(he[c_lo:c_lo + n_c], w_in[:, OFF_POOL:OFF_POOL + POOL_W])
    up[d_lo:d_lo + n_c] = seq_mask(zp)
    zs = _dot(he[c_lo:c_lo + n_c], w_in[:, OFF_SC:OFF_SC + 2 * SC_W])
    uc[d_lo:d_lo + n_c] = seq_mask(zs[:, :SC_W] * zs[:, SC_W:])

    merged = r["merged"]
    act_a, act_p, act_b, act_c = r["act_a"], r["act_p"], r["act_b"], r["act_c"]
    M0 = HALO_CONV

    def gate_merge(branch, act, w_out):
        for c in range(0, D_MODEL, MIX_CHUNK):
            g_off = OFF_G + branch * D_MODEL + c
            g = _sigmoid(_dot(he_main(), w_in[:, g_off:g_off + MIX_CHUNK]))
            val = g * _dot(act[...], w_out[:, c:c + MIX_CHUNK])
            if branch == 0:
                merged[:, c:c + MIX_CHUNK] = val
            else:
                merged[:, c:c + MIX_CHUNK] += val
            yield

    def conv_a():
        wca = r["w_conv_a"][...]
        b_conv, ln_g, ln_b = r["b_conv_a"][...], r["ln_g_a"][...], r["ln_b_a"][...]
        uas = r["uas"]
        for s in range(1, SUBLANES):
            uas[s - 1] = ua[s:s + TC]
        yield
        for t0 in range(0, T, ROW_CHUNK):
            acc = jnp.zeros((ROW_CHUNK, CONV_W), F32)
            for k in range(CONV_K):
                lo = M0 + t0 + k - CONV_K // 2
                s, base = lo % SUBLANES, lo - lo % SUBLANES
                win = ua[base:base + ROW_CHUNK] if s == 0 else uas[s - 1, base:base + ROW_CHUNK]
                acc = acc + wca[k:k + 1] * win
            acc = acc + b_conv
            mu = jnp.mean(acc, axis=-1, keepdims=True)
            cen = acc - mu
            var = jnp.mean(cen * cen, axis=-1, keepdims=True)
            y = (cen * lax.rsqrt(var + EPS)) * ln_g + ln_b
            act_a[t0:t0 + ROW_CHUNK] = (y * _sigmoid(y)).astype(BF16)
            yield

    def pool():
        lane = lax.broadcasted_iota(jnp.int32, (ROW_CHUNK, LANES), 1)
        low_half = lane < POOL_G
        for t0 in range(0, T, ROW_CHUNK):
            pos = pos0 + t0 + lax.broadcasted_iota(jnp.int32, (ROW_CHUNK, 1), 0)

            def inv_cnt(w):
                cnt = jnp.minimum(pos + w // 2, L) - jnp.maximum(pos - w // 2, 0)
                return 1.0 / cnt.astype(F32)

            def win(lt, off):
                lo = M0 + t0 + off
                return up[lo:lo + ROW_CHUNK, lt * LANES:(lt + 1) * LANES]

            for lt in range(POOL_W // LANES):
                w_small, w_big = POOL_SIZES[2 * lt], POOL_SIZES[2 * lt + 1]
                tok = win(lt, 0)
                s = win(lt, -1) + tok
                have = 2
                sums = {2: s}
                while have < w_big:
                    nxt = 2 * have
                    for o in range(have // 2, nxt // 2):
                        s = s + win(lt, -o - 1) + win(lt, o)
                    have = nxt
                    sums[have] = s
                pooled = jnp.where(low_half, sums[w_small] * inv_cnt(w_small),
                                   sums[w_big] * inv_cnt(w_big)) - tok
                act_p[t0:t0 + ROW_CHUNK, lt * LANES:(lt + 1) * LANES] = pooled.astype(BF16)
            yield

    def pool_map():
        ub = _dot(act_p[...], r["w_poolbd"][...]) * r["pool_scale"][...]
        act_b[...] = ub.astype(BF16)
        yield

    def conv_c():
        wsc = r["w_sc"][...]
        s_b = _dot(he_main(), w_in[:, OFF_SB:OFF_SB + SC_W])
        yield
        for t0 in range(0, T, ROW_CHUNK):
            acc = jnp.zeros((ROW_CHUNK, SC_W), F32)
            for k in range(SC_K):
                lo = M0 + t0 + k - SC_K // 2
                acc = acc + wsc[k:k + 1] * uc[lo:lo + ROW_CHUNK]
            act_c[t0:t0 + ROW_CHUNK] = (s_b[t0:t0 + ROW_CHUNK] * acc).astype(BF16)
            yield

    def rope(x, tab):
        n = x.shape[0]
        ln = lax.broadcasted_iota(jnp.int32, (n, LANES), 1)
        first = (ln % (HEAD_DIM // 2)) < (HEAD_DIM // 4)
        quarter = HEAD_DIM // 4
        partner = jnp.where(first, pltpu.roll(x, LANES - quarter, 1), pltpu.roll(x, quarter, 1))
        return x * tab[:, :LANES] + partner * tab[:, LANES:]

    def dup_heads(x):
        ln = lax.broadcasted_iota(jnp.int32, x.shape, 1)
        sw = pltpu.roll(x, HEAD_DIM, 1)
        lo = ln < HEAD_DIM
        return jnp.where(lo, x, sw), jnp.where(lo, sw, x)

    kd, vd, q_s, o_s = r["kd"], r["vd"], r["q"], r["o"]

    def qkv_proj():
        kv = _dot(he[...], w_in[:, OFF_K:OFF_K + 2 * KV_W])
        k, v = kv[:, :KV_W], kv[:, KV_W:]
        if halo:
            tab_e = jnp.concatenate([r["rtp"][...], r["rtm"][...], r["rtn"][...]], axis=0)
            k = rope(k, tab_e)
        else:
            kv_t = _dot_nt(r["w_kv_t"][...], he[...])
            r["k_out"][0] = kv_t[:KV_W]
            r["v_out"][0] = kv_t[KV_W:]
        for g, (kk, vv) in enumerate(zip(dup_heads(k), dup_heads(v))):
            kd[g] = kk.astype(BF16)
            vd[g] = vv.astype(BF16)
        yield
        if halo:
            for g, (kk, vv) in enumerate(zip(dup_heads(r["ck"][0]), dup_heads(r["cv"][0]))):
                r["kcd"][g] = kk.astype(BF16)
                r["vcd"][g] = vv.astype(BF16)
            yield
        scale = HEAD_DIM ** -0.5
        for p in range(ATT_W // LANES):
            qp = _dot(he_main(), w_in[:, OFF_Q + p * LANES:OFF_Q + (p + 1) * LANES])
            if halo:
                qp = rope(qp, r["rtm"][...])
            q_s[:, p * LANES:(p + 1) * LANES] = (qp * scale).astype(BF16)
            yield

    sink = r["sink"]
    lane_q = lax.broadcasted_iota(jnp.int32, (BQ, LANES), 1) < HEAD_DIM
    n_loc = 3 * WINDOW if halo else T

    def scores(j, g):
        rows = slice(j * BQ, (j + 1) * BQ)
        q0 = q_s[rows, (2 * g) * LANES:(2 * g + 1) * LANES]
        q1 = q_s[rows, (2 * g + 1) * LANES:(2 * g + 2) * LANES]
        zero = jnp.zeros_like(q0)
        qz = jnp.concatenate([jnp.where(lane_q, q0, zero), jnp.where(lane_q, q1, zero),
                              jnp.where(lane_q, zero, q0), jnp.where(lane_q, zero, q1)], axis=0)
        k_loc = kd[g, j * BQ:j * BQ + n_loc] if halo else kd[g]
        return _dot_nt(qz, k_loc), (_dot_nt(qz, r["kcd"][g]) if halo else None)

    def attention():
        blocks = [(j, g) for j in range(T // BQ) for g in range(N_KV)]
        nxt = scores(*blocks[0])
        for i, (j, g) in enumerate(blocks):
            (s_loc, s_ctx), nxt = nxt, (scores(*blocks[i + 1]) if i + 1 < len(blocks) else None)
            rows = slice(j * BQ, (j + 1) * BQ)
            if halo:
                col = lax.broadcasted_iota(jnp.int32, (BQ, WINDOW), 1)
                row = lax.broadcasted_iota(jnp.int32, (BQ, WINDOW), 0)
                qpos0 = pos0 + j * BQ
                ok_prev = col >= row + jnp.where(qpos0 >= WINDOW, 0, WINDOW)
                ok_next = col <= row - jnp.where(qpos0 + 2 * WINDOW <= L, 0, WINDOW)
            heads = (4 * g, 4 * g + 2, 4 * g + 1, 4 * g + 3)
            v_loc = vd[g, j * BQ:j * BQ + n_loc] if halo else vd[g]
            p_loc, p_ctx, inv_l = [], [], []
            for b, h in enumerate(heads):
                rb = slice(b * BQ, (b + 1) * BQ)
                sl = s_loc[rb]
                if halo:
                    sl = jnp.concatenate([jnp.where(ok_prev, sl[:, :WINDOW], NEG), sl[:, WINDOW:2 * WINDOW],
                                          jnp.where(ok_next, sl[:, 2 * WINDOW:], NEG)], axis=1)
                sk = sink[h]
                m = jnp.maximum(jnp.max(sl, axis=-1, keepdims=True), sk)
                if halo:
                    sc = s_ctx[rb]
                    m = jnp.maximum(m, jnp.max(sc, axis=-1, keepdims=True))
                pl_ = jnp.exp(sl - m)
                den = jnp.sum(pl_, axis=-1, keepdims=True) + jnp.exp(sk - m)
                p_loc.append(pl_.astype(BF16))
                if halo:
                    pc = jnp.exp(sc - m)
                    den = den + jnp.sum(pc, axis=-1, keepdims=True)
                    p_ctx.append(pc.astype(BF16))
                inv_l.append(1.0 / den)
            o = _dot(jnp.concatenate(p_loc, axis=0), v_loc)
            if halo:
                o = o + _dot(jnp.concatenate(p_ctx, axis=0), r["vcd"][g])
            ob = [o[b * BQ:(b + 1) * BQ] * inv_l[b] for b in range(GROUP)]
            o_s[rows, (2 * g) * LANES:(2 * g + 1) * LANES] = jnp.where(lane_q, ob[0], ob[2]).astype(BF16)
            o_s[rows, (2 * g + 1) * LANES:(2 * g + 2) * LANES] = jnp.where(lane_q, ob[1], ob[3]).astype(BF16)
            yield

    _run_together(conv_a(), qkv_proj())
    _run_together(pool(), gate_merge(0, act_a, r["w_a_out"]))
    _run_together(pool_map())
    _run_together(conv_c(), gate_merge(1, act_b, r["w_b_out"]))
    _run_together(attention(), gate_merge(2, act_c, r["w_c_out"]))
    _run_together(gate_merge(3, o_s, r["w_d_out"]))

    g_post = r["g_post"][...]
    for i in range(0, T, 256):
        m = _dot(merged[i:i + 256].astype(BF16), r["w_o"][...])
        r["x1"][0, i:i + 256] = r["xm"][0, i:i + 256] + gt1 * (_rms(m) * g_post)


def _run_together(*stages):
    stages = list(stages)
    while stages:
        for st in list(stages):
            try:
                next(st)
            except StopIteration:
                stages.remove(st)


def _layer_spec(arr, l):
    nd = arr.ndim - 1
    return pl.BlockSpec((None,) + arr.shape[1:], lambda b, j: (l,) + (0,) * nd, pipeline_mode=pl.Buffered(1))


def _mixer(x, mod, mod_row, lw, l, *, T, ctx_k=None, ctx_v=None, rope_tab=None):
    B, L, _ = x.shape
    halo = ctx_k is not None
    nt = L // T
    H = HALO_ATT if halo else 0
    E = T + 2 * H
    TC = T + 2 * HALO_CONV
    hb = T // HALO_ATT
    n_hb = L // HALO_ATT

    names, args, specs = [], [], []

    def add(name, arr, spec):
        names.append(name)
        args.append(arr)
        specs.append(spec)

    add("xm", x, pl.BlockSpec((1, T, D_MODEL), lambda b, j: (b, j, 0)))
    if halo:
        add("xp", x, pl.BlockSpec((1, HALO_ATT, D_MODEL), lambda b, j: (b, jnp.maximum(j * hb - 1, 0), 0)))
        add("xn", x, pl.BlockSpec((1, HALO_ATT, D_MODEL),
                                  lambda b, j: (b, jnp.minimum((j + 1) * hb, n_hb - 1), 0)))
    add("mod", mod, pl.BlockSpec((1, 3, D_MODEL), mod_row))
    for nm in ("g_pre", "g_post", "w_in", "w_conv_a", "b_conv_a", "ln_g_a", "ln_b_a", "w_a_out", "w_poolbd",
               "pool_scale", "w_b_out", "w_sc", "w_c_out"):
        add(nm, lw[nm], _layer_spec(lw[nm], l))
    add("sink", lw["sink"][l], pl.BlockSpec(memory_space=pltpu.SMEM))
    for nm in ("w_d_out", "w_o"):
        add(nm, lw[nm], _layer_spec(lw[nm], l))
    if not halo:
        add("w_kv_t", lw["w_kv_t"], _layer_spec(lw["w_kv_t"], l))
    if halo:
        kv_spec = pl.BlockSpec((1, None) + ctx_k.shape[2:], lambda b, j: (b, l, 0, 0))
        add("ck", ctx_k, kv_spec)
        add("cv", ctx_v, kv_spec)
        add("rtm", rope_tab, pl.BlockSpec((T, 2 * LANES), lambda b, j: (j, 0)))
        add("rtp", rope_tab, pl.BlockSpec((HALO_ATT, 2 * LANES), lambda b, j: (jnp.maximum(j * hb - 1, 0), 0)))
        add("rtn", rope_tab, pl.BlockSpec((HALO_ATT, 2 * LANES),
                                          lambda b, j: (jnp.minimum((j + 1) * hb, n_hb - 1), 0)))

    out_names = ["x1"]
    out_shape = [jax.ShapeDtypeStruct((B, L, D_MODEL), F32)]
    out_specs = [pl.BlockSpec((1, T, D_MODEL), lambda b, j: (b, j, 0))]
    if not halo:
        for nm in ("k_out", "v_out"):
            out_names.append(nm)
            out_shape.append(jax.ShapeDtypeStruct((B, KV_W, L), F32))
            out_specs.append(pl.BlockSpec((1, KV_W, T), lambda b, j: (b, 0, j)))

    scratch = [("he", pltpu.VMEM((E, D_MODEL), BF16)),
               ("ua", pltpu.VMEM((TC + SUBLANES, CONV_W), F32)),
               ("uas", pltpu.VMEM((SUBLANES - 1, TC, CONV_W), F32)),
               ("up", pltpu.VMEM((TC, POOL_W), F32)),
               ("uc", pltpu.VMEM((TC, SC_W), F32)),
               ("act_a", pltpu.VMEM((T, CONV_W), BF16)),
               ("act_p", pltpu.VMEM((T, POOL_W), BF16)),
               ("act_b", pltpu.VMEM((T, POOL_W), BF16)),
               ("act_c", pltpu.VMEM((T, SC_W), BF16)),
               ("q", pltpu.VMEM((T, ATT_W), BF16)),
               ("kd", pltpu.VMEM((N_KV, E, LANES), BF16)),
               ("vd", pltpu.VMEM((N_KV, E, LANES), BF16)),
               ("o", pltpu.VMEM((T, ATT_W), BF16)),
               ("merged", pltpu.VMEM((T, D_MODEL), F32))]
    if halo:
        n_ctx = ctx_k.shape[2]
        scratch += [("kcd", pltpu.VMEM((N_KV, n_ctx, LANES), BF16)),
                    ("vcd", pltpu.VMEM((N_KV, n_ctx, LANES), BF16))]

    all_names = tuple(names + out_names + [s[0] for s in scratch])
    kern = functools.partial(_mixer_kernel, names=all_names, T=T, L=L, halo=halo)
    return pl.pallas_call(
        kern,
        grid=(B, nt),
        in_specs=specs,
        out_specs=out_specs,
        out_shape=out_shape,
        scratch_shapes=[s[1] for s in scratch],
        compiler_params=pltpu.CompilerParams(dimension_semantics=("arbitrary", "arbitrary"),
                                             vmem_limit_bytes=VMEM_LIMIT_BYTES),
        name="mixer_latent" if halo else "mixer_context",
    )(*args)


def _ffn_kernel(x_ref, mod_ref, g_pre_ref, g_post_ref, w1_ref, w2_ref, o_ref, h_ref, hid_ref):
    x = x_ref[0]
    mod = mod_ref[0]
    sh2, sc2, gt2 = mod[0:1], mod[1:2], mod[2:3]
    h_ref[...] = ((_rms(x) * g_pre_ref[...]) * (1.0 + sc2) + sh2).astype(BF16)
    for c in range(0, D_FF, COL_CHUNK):
        a = _dot(h_ref[...], w1_ref[:, c:c + COL_CHUNK])
        hid_ref[:, c:c + COL_CHUNK] = jnp.square(jnp.maximum(a, 0.0)).astype(BF16)
    f = _dot(hid_ref[...], w2_ref[...])
    o_ref[0] = x + gt2 * (_rms(f) * g_post_ref[...])


def _ffn(x, mod, mod_row, lw, l, *, T):
    B, L, _ = x.shape
    return pl.pallas_call(
        _ffn_kernel,
        grid=(B, L // T),
        in_specs=[
            pl.BlockSpec((1, T, D_MODEL), lambda b, j: (b, j, 0)),
            pl.BlockSpec((1, 3, D_MODEL), mod_row),
            _layer_spec(lw["g_pre_ffn"], l),
            _layer_spec(lw["g_post_ffn"], l),
            _layer_spec(lw["w_ff1"], l),
            _layer_spec(lw["w_ff2"], l),
        ],
        out_specs=pl.BlockSpec((1, T, D_MODEL), lambda b, j: (b, j, 0)),
        out_shape=jax.ShapeDtypeStruct(x.shape, F32),
        scratch_shapes=[pltpu.VMEM((T, D_MODEL), BF16), pltpu.VMEM((T, D_FF), BF16)],
        compiler_params=pltpu.CompilerParams(dimension_semantics=("arbitrary", "arbitrary"),
                                             vmem_limit_bytes=VMEM_LIMIT_BYTES),
        name="ffn",
    )(x, mod, lw["g_pre_ffn"], lw["g_post_ffn"], lw["w_ff1"], lw["w_ff2"])


def _rope_table(n):
    half = HEAD_DIM // 4
    freq = (np.float32(ROPE_BASE) ** (-np.arange(half, dtype=np.float32) / np.float32(half))).astype(np.float32)
    pos = np.arange(n)
    ang_r = ((pos // GRID_W).astype(np.float32)[:, None] * freq[None, :]).astype(np.float32)
    ang_c = ((pos % GRID_W).astype(np.float32)[:, None] * freq[None, :]).astype(np.float32)
    cos = np.concatenate([np.cos(ang_r)] * 2 + [np.cos(ang_c)] * 2, axis=-1)
    sin = np.concatenate([-np.sin(ang_r), np.sin(ang_r), -np.sin(ang_c), np.sin(ang_c)], axis=-1)
    return jnp.asarray(np.concatenate([cos, cos, sin, sin], axis=-1), dtype=F32)


def _block_diag(w_pool):
    n = w_pool.shape[0]
    rows = [jnp.concatenate([w_pool[g] if h == g else jnp.zeros_like(w_pool[g]) for h in range(n)], axis=1)
            for g in range(n)]
    return jnp.concatenate(rows, axis=0)


def kernel(x_prompt, x_sample, cache_k, cache_v, c, c_ctx, w_ada, b_ada, g_pre_mix, g_post_mix, g_pre_ffn,
           g_post_ffn, w_in, w_conv_a, b_conv_a, ln_g_a, ln_b_a, w_a_out, w_pool, pool_scale, w_b_out, w_sc,
           w_c_out, sink, w_d_out, w_o, w_ff1, w_ff2):
    n_dec = c.shape[0]
    ctx_row = n_dec
    cond = jnp.concatenate([c, c_ctx[None, :], jnp.zeros((ADA_ROWS - n_dec - 1, D_MODEL), F32)], axis=0)
    mod = _ada(cond, w_ada, b_ada).reshape(DEPTH, ADA_ROWS, 6, D_MODEL)

    def rows(a):
        return a.reshape(DEPTH, 1, -1)

    lw = {
        "g_pre": rows(g_pre_mix), "g_post": rows(g_post_mix),
        "g_pre_ffn": rows(g_pre_ffn), "g_post_ffn": rows(g_post_ffn),
        "w_in": w_in.astype(BF16),
        "w_conv_a": w_conv_a, "b_conv_a": rows(b_conv_a), "ln_g_a": rows(ln_g_a), "ln_b_a": rows(ln_b_a),
        "w_a_out": w_a_out.astype(BF16),
        "w_poolbd": jnp.stack([_block_diag(w_pool[l]) for l in range(DEPTH)]).astype(BF16),
        "pool_scale": rows(pool_scale), "w_b_out": w_b_out.astype(BF16),
        "w_sc": w_sc, "w_c_out": w_c_out.astype(BF16), "sink": sink,
        "w_d_out": w_d_out.astype(BF16), "w_o": w_o.astype(BF16),
        "w_kv_t": jnp.swapaxes(w_in[:, :, OFF_K:OFF_K + 2 * KV_W], 1, 2).astype(BF16),
        "w_ff1": w_ff1.astype(BF16), "w_ff2": w_ff2.astype(BF16),
    }
    rope_tab = _rope_table(x_sample.shape[1])
    n_b, n_s = x_prompt.shape[:2]
    ck = cache_k.reshape(cache_k.shape[:3] + (KV_W,))
    cv = cache_v.reshape(cache_v.shape[:3] + (KV_W,))

    xp, xs = x_prompt, x_sample
    ks, vs = [], []
    for l in range(DEPTH):
        mod_mix, mod_ffn = mod[l, :, 0:3], mod[l, :, 3:6]
        xp, k, v = _mixer(xp, mod_mix, lambda b, j: (ctx_row, 0, 0), lw, l, T=n_s)
        xp = _ffn(xp.reshape(n_b // 2, 2 * n_s, D_MODEL), mod_ffn, lambda b, j: (ctx_row, 0, 0), lw, l,
                  T=2 * n_s).reshape(n_b, n_s, D_MODEL)
        ks.append(k.reshape(n_b, N_KV, HEAD_DIM, n_s))
        vs.append(v.reshape(n_b, N_KV, HEAD_DIM, n_s))
        xs = _mixer(xs, mod_mix, lambda b, j: (b, 0, 0), lw, l, T=512, ctx_k=ck, ctx_v=cv,
                    rope_tab=rope_tab)[0]
        xs = _ffn(xs, mod_ffn, lambda b, j: (b, 0, 0), lw, l, T=512)
    new_k = jnp.stack(ks, axis=0).transpose(1, 0, 4, 2, 3)
    new_v = jnp.stack(vs, axis=0).transpose(1, 0, 4, 2, 3)
    return (xp, xs, new_k, new_v)
```

```python
import functools

import jax
import jax.numpy as jnp
import numpy as np
from jax import lax
from jax.experimental import pallas as pl
from jax.experimental.pallas import tpu as pltpu

F32 = jnp.float32
BF16 = jnp.bfloat16

D_MODEL = 1024
DEPTH = 2
GRID_W = 64
N_HEADS = 8
N_KV = 2
HEAD_DIM = 64
GROUP = N_HEADS // N_KV
ATT_W = N_HEADS * HEAD_DIM
KV_W = N_KV * HEAD_DIM
WINDOW = 128
ROPE_BASE = 10000.0
CONV_W = D_MODEL // 4
CONV_K = 31
POOL_W = D_MODEL // 4
POOL_SIZES = (2, 4, 8, 16)
POOL_G = POOL_W // len(POOL_SIZES)
SC_W = D_MODEL // 4
SC_K = 3
N_BRANCH = 4
D_FF = 4 * D_MODEL
EPS = 1e-6
NEG = -1e30
IN_W = 2 * CONV_W + POOL_W + 3 * SC_W + ATT_W + 2 * KV_W + N_BRANCH * D_MODEL

OFF_AVAL = 0
OFF_AGATE = OFF_AVAL + CONV_W
OFF_POOL = OFF_AGATE + CONV_W
OFF_SB = OFF_POOL + POOL_W
OFF_SC = OFF_SB + SC_W
OFF_SH = OFF_SC + SC_W
OFF_Q = OFF_SH + SC_W
OFF_K = OFF_Q + ATT_W
OFF_V = OFF_K + KV_W
OFF_G = OFF_V + KV_W

LANES = 128
SUBLANES = 8
BF16_ROWS = 16
VMEM_LIMIT_BYTES = 56 * 1024 * 1024

HALO_ATT = WINDOW
HALO_CONV = 16
FFN_ROWS = 1024
CTX_SEQS = 2
SEQ_SKEW = 24
ROW_CHUNK = 64
COL_CHUNK = 512
MIX_CHUNK = 256
N_EARLY = 2
ADA_ROWS = 8
ADA_COLS = 1024


def _dot(a, b):
    return jnp.dot(a, b, preferred_element_type=F32)


def _dot_nt(a, b):
    return lax.dot_general(a, b, (((1,), (1,)), ((), ())), preferred_element_type=F32)


def _rms(x):
    return x * lax.rsqrt(jnp.mean(x * x, axis=-1, keepdims=True) + EPS)


def _sigmoid(x):
    return 0.5 * jnp.tanh(0.5 * x) + 0.5


def _ada_kernel(cond_ref, w_ref, b_ref, o_ref):
    c = cond_ref[...]
    s = (c * _sigmoid(c)).astype(BF16)
    o_ref[0] = _dot(s, w_ref[0].astype(BF16)) + b_ref[0]


def _ada(cond, w_ada, b_ada):
    n_col = w_ada.shape[-1] // ADA_COLS
    return pl.pallas_call(
        _ada_kernel,
        grid=(DEPTH, n_col),
        in_specs=[
            pl.BlockSpec((ADA_ROWS, D_MODEL), lambda l, c: (0, 0)),
            pl.BlockSpec((1, D_MODEL, ADA_COLS), lambda l, c: (l, 0, c)),
            pl.BlockSpec((1, 1, ADA_COLS), lambda l, c: (l, 0, c)),
        ],
        out_specs=pl.BlockSpec((1, ADA_ROWS, ADA_COLS), lambda l, c: (l, 0, c)),
        out_shape=jax.ShapeDtypeStruct((DEPTH, ADA_ROWS, w_ada.shape[-1]), F32),
        name="ada",
    )(cond, w_ada, b_ada.reshape(DEPTH, 1, -1))


def _mixer_kernel(*refs, names, per_seq, n_seq, T, L, halo):
    tiles = [_mixer_tile({nm: (ref.at[s] if nm in per_seq else ref) for nm, ref in zip(names, refs)}, T, L, halo)
             for s in range(n_seq)]
    live, step = [], 0
    while tiles or live:
        if tiles and step % SEQ_SKEW == 0:
            live.append(tiles.pop(0))
        for t in list(live):
            if next(t, _DONE) is _DONE:
                live.remove(t)
        step += 1


def _mixer_tile(r, T, L, halo):
    H = HALO_ATT if halo else 0
    E = T + 2 * H
    TC = T + 2 * HALO_CONV
    BQ = WINDOW if halo else T
    pos0 = pl.program_id(1) * T

    mod = r["mod"][0]
    sh1, sc1, gt1 = mod[0:1], mod[1:2], mod[2:3]
    g_pre = r["g_pre"][...]
    he = r["he"]
    w_in = r["w_in"]

    g_mod = g_pre * (1.0 + sc1)

    def norm_mod(x):
        return (_rms(x) * g_mod + sh1).astype(BF16)

    if halo:
        he[0:H] = norm_mod(r["xp"][...])
        he[H + T:E] = norm_mod(r["xn"][...])
    for i in range(0, T, 256):
        he[H + i:H + i + 256] = norm_mod(r["xm"][i:i + 256])
    yield

    def he_main():
        return he[H:H + T]

    ua, up, uc = r["ua"], r["up"], r["uc"]
    if halo:
        c_lo, n_c, d_lo = H - HALO_CONV, TC, 0
        rp = pos0 - HALO_CONV + lax.broadcasted_iota(jnp.int32, (TC, 1), 0)
        in_seq = (rp >= 0) & (rp < L)
    else:
        c_lo, n_c, d_lo = 0, T, HALO_CONV
        zeros = jnp.zeros((HALO_CONV, CONV_W), F32)
        for buf in (ua, up, uc):
            buf[0:HALO_CONV] = zeros
            buf[HALO_CONV + T:TC] = zeros
    ua[TC:TC + SUBLANES] = jnp.zeros((SUBLANES, CONV_W), F32)

    def seq_mask(v):
        return jnp.where(in_seq, v, 0.0) if halo else v

    za = _dot(he[c_lo:c_lo + n_c], w_in[:, OFF_AVAL:OFF_AVAL + 2 * CONV_W])
    ua[d_lo:d_lo + n_c] = seq_mask(za[:, :CONV_W] * _sigmoid(za[:, CONV_W:]))
    yield
    zp = _dot(he[c_lo:c_lo + n_c], w_in[:, OFF_POOL:OFF_POOL + POOL_W])
    up[d_lo:d_lo + n_c] = seq_mask(zp)
    yield
    zs = _dot(he[c_lo:c_lo + n_c], w_in[:, OFF_SC:OFF_SC + 2 * SC_W])
    uc[d_lo:d_lo + n_c] = seq_mask(zs[:, :SC_W] * zs[:, SC_W:])
    yield

    merged = r["merged"]
    act_a, act_p, act_b, act_c = r["act_a"], r["act_p"], r["act_b"], r["act_c"]
    M0 = HALO_CONV

    g_early = r["g_early"]

    def gate_logits(branch, c):
        g_off = OFF_G + branch * D_MODEL + c
        return _dot(he_main(), w_in[:, g_off:g_off + MIX_CHUNK])

    def gates_early(branch):
        for c in range(0, D_MODEL, MIX_CHUNK):
            g_early[branch, :, c:c + MIX_CHUNK] = gate_logits(branch, c)
            yield

    def gate_merge(branch, act, w_out):
        for c in range(0, D_MODEL, MIX_CHUNK):
            g = _sigmoid(g_early[branch, :, c:c + MIX_CHUNK] if branch < N_EARLY else gate_logits(branch, c))
            val = g * _dot(act[...], w_out[:, c:c + MIX_CHUNK])
            if branch == 0:
                merged[:, c:c + MIX_CHUNK] = val
            else:
                merged[:, c:c + MIX_CHUNK] += val
            yield

    def conv_a():
        wca = r["w_conv_a"][...]
        b_conv, ln_g, ln_b = r["b_conv_a"][...], r["ln_g_a"][...], r["ln_b_a"][...]
        uas = r["uas"]
        for s in range(1, SUBLANES):
            uas[s - 1] = ua[s:s + TC]
        yield
        for t0 in range(0, T, ROW_CHUNK):
            acc = jnp.zeros((ROW_CHUNK, CONV_W), F32)
            for k in range(CONV_K):
                lo = M0 + t0 + k - CONV_K // 2
                s, base = lo % SUBLANES, lo - lo % SUBLANES
                win = ua[base:base + ROW_CHUNK] if s == 0 else uas[s - 1, base:base + ROW_CHUNK]
                acc = acc + wca[k:k + 1] * win
            acc = acc + b_conv
            mu = jnp.mean(acc, axis=-1, keepdims=True)
            cen = acc - mu
            var = jnp.mean(cen * cen, axis=-1, keepdims=True)
            y = (cen * lax.rsqrt(var + EPS)) * ln_g + ln_b
            act_a[t0:t0 + ROW_CHUNK] = (y * _sigmoid(y)).astype(BF16)
            yield

    def pool():
        lane = lax.broadcasted_iota(jnp.int32, (ROW_CHUNK, LANES), 1)
        low_half = lane < POOL_G
        for t0 in range(0, T, ROW_CHUNK):
            pos = pos0 + t0 + lax.broadcasted_iota(jnp.int32, (ROW_CHUNK, 1), 0)

            def inv_cnt(w):
                cnt = jnp.minimum(pos + w // 2, L) - jnp.maximum(pos - w // 2, 0)
                return 1.0 / cnt.astype(F32)

            def win(lt, off):
                lo = M0 + t0 + off
                return up[lo:lo + ROW_CHUNK, lt * LANES:(lt + 1) * LANES]

            for lt in range(POOL_W // LANES):
                w_small, w_big = POOL_SIZES[2 * lt], POOL_SIZES[2 * lt + 1]
                tok = win(lt, 0)
                s = win(lt, -1) + tok
                have = 2
                sums = {2: s}
                while have < w_big:
                    nxt = 2 * have
                    for o in range(have // 2, nxt // 2):
                        s = s + win(lt, -o - 1) + win(lt, o)
                    have = nxt
                    sums[have] = s
                pooled = jnp.where(low_half, sums[w_small] * inv_cnt(w_small),
                                   sums[w_big] * inv_cnt(w_big)) - tok
                act_p[t0:t0 + ROW_CHUNK, lt * LANES:(lt + 1) * LANES] = pooled.astype(BF16)
            yield

    def pool_map():
        ub = _dot(act_p[...], r["w_poolbd"][...]) * r["pool_scale"][...]
        act_b[...] = ub.astype(BF16)
        yield

    def conv_c():
        wsc = r["w_sc"][...]
        s_b = _dot(he_main(), w_in[:, OFF_SB:OFF_SB + SC_W])
        yield
        for t0 in range(0, T, ROW_CHUNK):
            acc = jnp.zeros((ROW_CHUNK, SC_W), F32)
            for k in range(SC_K):
                lo = M0 + t0 + k - SC_K // 2
                acc = acc + wsc[k:k + 1] * uc[lo:lo + ROW_CHUNK]
            act_c[t0:t0 + ROW_CHUNK] = (s_b[t0:t0 + ROW_CHUNK] * acc).astype(BF16)
            yield

    def rope(x, tab):
        n = x.shape[0]
        ln = lax.broadcasted_iota(jnp.int32, (n, LANES), 1)
        first = (ln % (HEAD_DIM // 2)) < (HEAD_DIM // 4)
        quarter = HEAD_DIM // 4
        partner = jnp.where(first, pltpu.roll(x, LANES - quarter, 1), pltpu.roll(x, quarter, 1))
        return x * tab[:, :LANES] + partner * tab[:, LANES:]

    def dup_heads(x):
        ln = lax.broadcasted_iota(jnp.int32, x.shape, 1)
        sw = pltpu.roll(x, HEAD_DIM, 1)
        lo = ln < HEAD_DIM
        return jnp.where(lo, x, sw), jnp.where(lo, sw, x)

    kd, vd, q_s, o_s = r["kd"], r["vd"], r["q"], r["o"]

    def qkv_proj():
        kv = _dot(he[...], w_in[:, OFF_K:OFF_K + 2 * KV_W])
        k, v = kv[:, :KV_W], kv[:, KV_W:]
        if halo:
            tab_e = jnp.concatenate([r["rtp"][...], r["rtm"][...], r["rtn"][...]], axis=0)
            k = rope(k, tab_e)
        else:
            kv_t = _dot_nt(r["w_kv_t"][...], he[...])
            r["k_out"][...] = kv_t[:KV_W]
            r["v_out"][...] = kv_t[KV_W:]
        for g, (kk, vv) in enumerate(zip(dup_heads(k), dup_heads(v))):
            kd[g] = kk.astype(BF16)
            vd[g] = vv.astype(BF16)
        yield
        if halo:
            for g, (kk, vv) in enumerate(zip(dup_heads(r["ck"][...]), dup_heads(r["cv"][...]))):
                r["kcd"][g] = kk.astype(BF16)
                r["vcd"][g] = vv.astype(BF16)
            yield
        scale = HEAD_DIM ** -0.5
        for p in range(ATT_W // LANES):
            qp = _dot(he_main(), w_in[:, OFF_Q + p * LANES:OFF_Q + (p + 1) * LANES])
            if halo:
                qp = rope(qp, r["rtm"][...])
            q_s[:, p * LANES:(p + 1) * LANES] = (qp * scale).astype(BF16)
            yield

    sink = r["sink"]
    lane_q = lax.broadcasted_iota(jnp.int32, (BQ, LANES), 1) < HEAD_DIM
    n_loc = 3 * WINDOW if halo else T

    def scores(j, g):
        rows = slice(j * BQ, (j + 1) * BQ)
        q0 = q_s[rows, (2 * g) * LANES:(2 * g + 1) * LANES]
        q1 = q_s[rows, (2 * g + 1) * LANES:(2 * g + 2) * LANES]
        zero = jnp.zeros_like(q0)
        qz = jnp.concatenate([jnp.where(lane_q, q0, zero), jnp.where(lane_q, q1, zero),
                              jnp.where(lane_q, zero, q0), jnp.where(lane_q, zero, q1)], axis=0)
        k_loc = kd[g, j * BQ:j * BQ + n_loc] if halo else kd[g]
        return _dot_nt(qz, k_loc), (_dot_nt(qz, r["kcd"][g]) if halo else None)

    def attention():
        blocks = [(j, g) for j in range(T // BQ) for g in range(N_KV)]
        nxt = scores(*blocks[0])
        for i, (j, g) in enumerate(blocks):
            (s_loc, s_ctx), nxt = nxt, (scores(*blocks[i + 1]) if i + 1 < len(blocks) else None)
            rows = slice(j * BQ, (j + 1) * BQ)
            if halo:
                col = lax.broadcasted_iota(jnp.int32, (BQ, WINDOW), 1)
                row = lax.broadcasted_iota(jnp.int32, (BQ, WINDOW), 0)
                qpos0 = pos0 + j * BQ
                ok_prev = col >= row + jnp.where(qpos0 >= WINDOW, 0, WINDOW)
                ok_next = col <= row - jnp.where(qpos0 + 2 * WINDOW <= L, 0, WINDOW)
            heads = (4 * g, 4 * g + 2, 4 * g + 1, 4 * g + 3)
            v_loc = vd[g, j * BQ:j * BQ + n_loc] if halo else vd[g]
            p_loc, p_ctx, inv_l = [], [], []
            for b, h in enumerate(heads):
                rb = slice(b * BQ, (b + 1) * BQ)
                sl = s_loc[rb]
                if halo:
                    sl = jnp.concatenate([jnp.where(ok_prev, sl[:, :WINDOW], NEG), sl[:, WINDOW:2 * WINDOW],
                                          jnp.where(ok_next, sl[:, 2 * WINDOW:], NEG)], axis=1)
                sk = sink[h]
                m = jnp.maximum(jnp.max(sl, axis=-1, keepdims=True), sk)
                if halo:
                    sc = s_ctx[rb]
                    m = jnp.maximum(m, jnp.max(sc, axis=-1, keepdims=True))
                pl_ = jnp.exp(sl - m)
                den = jnp.sum(pl_, axis=-1, keepdims=True) + jnp.exp(sk - m)
                p_loc.append(pl_.astype(BF16))
                if halo:
                    pc = jnp.exp(sc - m)
                    den = den + jnp.sum(pc, axis=-1, keepdims=True)
                    p_ctx.append(pc.astype(BF16))
                inv_l.append(1.0 / den)
            o = _dot(jnp.concatenate(p_loc, axis=0), v_loc)
            if halo:
                o = o + _dot(jnp.concatenate(p_ctx, axis=0), r["vcd"][g])
            ob = [o[b * BQ:(b + 1) * BQ] * inv_l[b] for b in range(GROUP)]
            o_s[rows, (2 * g) * LANES:(2 * g + 1) * LANES] = jnp.where(lane_q, ob[0], ob[2]).astype(BF16)
            o_s[rows, (2 * g + 1) * LANES:(2 * g + 2) * LANES] = jnp.where(lane_q, ob[1], ob[3]).astype(BF16)
            yield

    yield from _together(conv_a(), qkv_proj(), *[gates_early(b) for b in range(N_EARLY)])
    yield from _together(pool(), gate_merge(0, act_a, r["w_a_out"]))
    yield from pool_map()
    yield from _together(conv_c(), gate_merge(1, act_b, r["w_b_out"]))
    yield from _together(attention(), gate_merge(2, act_c, r["w_c_out"]))
    yield from gate_merge(3, o_s, r["w_d_out"])

    g_out = gt1 * r["g_post"][...]
    for i in range(0, T, 256):
        m = _dot(merged[i:i + 256].astype(BF16), r["w_o"][...])
        r["x1"][i:i + 256] = r["xm"][i:i + 256] + _rms(m) * g_out
        yield


_DONE = object()


def _together(*stages):
    stages = list(stages)
    while stages:
        for st in list(stages):
            if next(st, _DONE) is _DONE:
                stages.remove(st)
            else:
                yield


def _layer_spec(arr, l):
    nd = arr.ndim - 1
    return pl.BlockSpec((None,) + arr.shape[1:], lambda b, j: (l,) + (0,) * nd, pipeline_mode=pl.Buffered(1))


def _mixer(x, mod, mod_row, lw, l, *, T, n_seq=1, ctx_k=None, ctx_v=None, rope_tab=None):
    B, L, _ = x.shape
    halo = ctx_k is not None
    nt = L // T
    H = HALO_ATT if halo else 0
    E = T + 2 * H
    TC = T + 2 * HALO_CONV
    hb = T // HALO_ATT
    n_hb = L // HALO_ATT

    names, args, specs = [], [], []

    def add(name, arr, spec):
        names.append(name)
        args.append(arr)
        specs.append(spec)

    add("xm", x, pl.BlockSpec((n_seq, T, D_MODEL), lambda b, j: (b, j, 0)))
    if halo:
        add("xp", x, pl.BlockSpec((n_seq, HALO_ATT, D_MODEL), lambda b, j: (b, jnp.maximum(j * hb - 1, 0), 0)))
        add("xn", x, pl.BlockSpec((n_seq, HALO_ATT, D_MODEL),
                                  lambda b, j: (b, jnp.minimum((j + 1) * hb, n_hb - 1), 0)))
    add("mod", mod, pl.BlockSpec((1, 3, D_MODEL), mod_row))
    for nm in ("g_pre", "g_post", "w_in", "w_conv_a", "b_conv_a", "ln_g_a", "ln_b_a", "w_a_out", "w_poolbd",
               "pool_scale", "w_b_out", "w_sc", "w_c_out"):
        add(nm, lw[nm], _layer_spec(lw[nm], l))
    add("sink", lw["sink"][l], pl.BlockSpec(memory_space=pltpu.SMEM))
    for nm in ("w_d_out", "w_o"):
        add(nm, lw[nm], _layer_spec(lw[nm], l))
    if not halo:
        add("w_kv_t", lw["w_kv_t"], _layer_spec(lw["w_kv_t"], l))
    if halo:
        kv_spec = pl.BlockSpec((n_seq, None) + ctx_k.shape[2:], lambda b, j: (b, l, 0, 0))
        add("ck", ctx_k, kv_spec)
        add("cv", ctx_v, kv_spec)
        add("rtm", rope_tab, pl.BlockSpec((T, 2 * LANES), lambda b, j: (j, 0)))
        add("rtp", rope_tab, pl.BlockSpec((HALO_ATT, 2 * LANES), lambda b, j: (jnp.maximum(j * hb - 1, 0), 0)))
        add("rtn", rope_tab, pl.BlockSpec((HALO_ATT, 2 * LANES),
                                          lambda b, j: (jnp.minimum((j + 1) * hb, n_hb - 1), 0)))

    out_names = ["x1"]
    out_shape = [jax.ShapeDtypeStruct((B, L, D_MODEL), F32)]
    out_specs = [pl.BlockSpec((n_seq, T, D_MODEL), lambda b, j: (b, j, 0))]
    if not halo:
        for nm in ("k_out", "v_out"):
            out_names.append(nm)
            out_shape.append(jax.ShapeDtypeStruct((B, KV_W, L), F32))
            out_specs.append(pl.BlockSpec((n_seq, KV_W, T), lambda b, j: (b, 0, j)))

    scratch = [("he", pltpu.VMEM((E, D_MODEL), BF16)),
               ("ua", pltpu.VMEM((TC + SUBLANES, CONV_W), F32)),
               ("uas", pltpu.VMEM((SUBLANES - 1, TC, CONV_W), F32)),
               ("up", pltpu.VMEM((TC, POOL_W), F32)),
               ("uc", pltpu.VMEM((TC, SC_W), F32)),
               ("act_a", pltpu.VMEM((T, CONV_W), BF16)),
               ("act_p", pltpu.VMEM((T, POOL_W), BF16)),
               ("act_b", pltpu.VMEM((T, POOL_W), BF16)),
               ("act_c", pltpu.VMEM((T, SC_W), BF16)),
               ("q", pltpu.VMEM((T, ATT_W), BF16)),
               ("kd", pltpu.VMEM((N_KV, E, LANES), BF16)),
               ("vd", pltpu.VMEM((N_KV, E, LANES), BF16)),
               ("o", pltpu.VMEM((T, ATT_W), BF16)),
               ("merged", pltpu.VMEM((T, D_MODEL), F32)),
               ("g_early", pltpu.VMEM((N_EARLY, T, D_MODEL), F32))]
    if halo:
        n_ctx = ctx_k.shape[2]
        scratch += [("kcd", pltpu.VMEM((N_KV, n_ctx, LANES), BF16)),
                    ("vcd", pltpu.VMEM((N_KV, n_ctx, LANES), BF16))]

    scratch = [(nm, pltpu.VMEM((n_seq,) + tuple(buf.shape), buf.dtype)) for nm, buf in scratch]
    per_seq = frozenset(["xm", "xp", "xn", "ck", "cv"] + out_names + [s[0] for s in scratch])
    all_names = tuple(names + out_names + [s[0] for s in scratch])
    kern = functools.partial(_mixer_kernel, names=all_names, per_seq=per_seq, n_seq=n_seq, T=T, L=L, halo=halo)
    return pl.pallas_call(
        kern,
        grid=(B // n_seq, nt),
        in_specs=specs,
        out_specs=out_specs,
        out_shape=out_shape,
        scratch_shapes=[s[1] for s in scratch],
        compiler_params=pltpu.CompilerParams(dimension_semantics=("arbitrary", "arbitrary"),
                                             vmem_limit_bytes=VMEM_LIMIT_BYTES),
        name="mixer_latent" if halo else "mixer_context",
    )(*args)


def _ffn_kernel(x_ref, mod_ref, g_pre_ref, g_post_ref, w1_ref, w2_ref, o_ref, h_ref, hid_ref):
    x = x_ref[0]
    mod = mod_ref[0]
    sh2, sc2, gt2 = mod[0:1], mod[1:2], mod[2:3]
    h_ref[...] = (_rms(x) * (g_pre_ref[...] * (1.0 + sc2)) + sh2).astype(BF16)
    for c in range(0, D_FF, COL_CHUNK):
        a = _dot(h_ref[...], w1_ref[:, c:c + COL_CHUNK])
        hid_ref[:, c:c + COL_CHUNK] = jnp.square(jnp.maximum(a, 0.0)).astype(BF16)
    f = _dot(hid_ref[...], w2_ref[...])
    o_ref[0] = x + _rms(f) * (gt2 * g_post_ref[...])


def _ffn(x, mod, mod_row, lw, l, *, T):
    B, L, _ = x.shape
    return pl.pallas_call(
        _ffn_kernel,
        grid=(B, L // T),
        in_specs=[
            pl.BlockSpec((1, T, D_MODEL), lambda b, j: (b, j, 0)),
            pl.BlockSpec((1, 3, D_MODEL), mod_row),
            _layer_spec(lw["g_pre_ffn"], l),
            _layer_spec(lw["g_post_ffn"], l),
            _layer_spec(lw["w_ff1"], l),
            _layer_spec(lw["w_ff2"], l),
        ],
        out_specs=pl.BlockSpec((1, T, D_MODEL), lambda b, j: (b, j, 0)),
        out_shape=jax.ShapeDtypeStruct(x.shape, F32),
        scratch_shapes=[pltpu.VMEM((T, D_MODEL), BF16), pltpu.VMEM((T, D_FF), BF16)],
        compiler_params=pltpu.CompilerParams(dimension_semantics=("arbitrary", "arbitrary"),
                                             vmem_limit_bytes=VMEM_LIMIT_BYTES),
        name="ffn",
    )(x, mod, lw["g_pre_ffn"], lw["g_post_ffn"], lw["w_ff1"], lw["w_ff2"])


def _rope_table(n):
    half = HEAD_DIM // 4
    freq = (np.float32(ROPE_BASE) ** (-np.arange(half, dtype=np.float32) / np.float32(half))).astype(np.float32)
    pos = np.arange(n)
    ang_r = ((pos // GRID_W).astype(np.float32)[:, None] * freq[None, :]).astype(np.float32)
    ang_c = ((pos % GRID_W).astype(np.float32)[:, None] * freq[None, :]).astype(np.float32)
    cos = np.concatenate([np.cos(ang_r)] * 2 + [np.cos(ang_c)] * 2, axis=-1)
    sin = np.concatenate([-np.sin(ang_r), np.sin(ang_r), -np.sin(ang_c), np.sin(ang_c)], axis=-1)
    return jnp.asarray(np.concatenate([cos, cos, sin, sin], axis=-1), dtype=F32)


def _block_diag(w_pool):
    n = w_pool.shape[0]
    rows = [jnp.concatenate([w_pool[g] if h == g else jnp.zeros_like(w_pool[g]) for h in range(n)], axis=1)
            for g in range(n)]
    return jnp.concatenate(rows, axis=0)


def kernel(x_prompt, x_sample, cache_k, cache_v, c, c_ctx, w_ada, b_ada, g_pre_mix, g_post_mix, g_pre_ffn,
           g_post_ffn, w_in, w_conv_a, b_conv_a, ln_g_a, ln_b_a, w_a_out, w_pool, pool_scale, w_b_out, w_sc,
           w_c_out, sink, w_d_out, w_o, w_ff1, w_ff2):
    n_dec = c.shape[0]
    ctx_row = n_dec
    cond = jnp.concatenate([c, c_ctx[None, :], jnp.zeros((ADA_ROWS - n_dec - 1, D_MODEL), F32)], axis=0)
    mod = _ada(cond, w_ada, b_ada).reshape(DEPTH, ADA_ROWS, 6, D_MODEL)

    def rows(a):
        return a.reshape(DEPTH, 1, -1)

    lw = {
        "g_pre": rows(g_pre_mix), "g_post": rows(g_post_mix),
        "g_pre_ffn": rows(g_pre_ffn), "g_post_ffn": rows(g_post_ffn),
        "w_in": w_in.astype(BF16),
        "w_conv_a": w_conv_a, "b_conv_a": rows(b_conv_a), "ln_g_a": rows(ln_g_a), "ln_b_a": rows(ln_b_a),
        "w_a_out": w_a_out.astype(BF16),
        "w_poolbd": jnp.stack([_block_diag(w_pool[l]) for l in range(DEPTH)]).astype(BF16),
        "pool_scale": rows(pool_scale), "w_b_out": w_b_out.astype(BF16),
        "w_sc": w_sc, "w_c_out": w_c_out.astype(BF16), "sink": sink,
        "w_d_out": w_d_out.astype(BF16), "w_o": w_o.astype(BF16),
        "w_kv_t": jnp.swapaxes(w_in[:, :, OFF_K:OFF_K + 2 * KV_W], 1, 2).astype(BF16),
        "w_ff1": w_ff1.astype(BF16), "w_ff2": w_ff2.astype(BF16),
    }
    rope_tab = _rope_table(x_sample.shape[1])
    n_b, n_s = x_prompt.shape[:2]
    ck = cache_k.reshape(cache_k.shape[:3] + (KV_W,))
    cv = cache_v.reshape(cache_v.shape[:3] + (KV_W,))

    xp, xs = x_prompt, x_sample
    ks, vs = [], []
    for l in range(DEPTH):
        mod_mix, mod_ffn = mod[l, :, 0:3], mod[l, :, 3:6]
        xp, k, v = _mixer(xp, mod_mix, lambda b, j: (ctx_row, 0, 0), lw, l, T=n_s, n_seq=CTX_SEQS)
        xp = _ffn(xp.reshape(-1, FFN_ROWS, D_MODEL), mod_ffn, lambda b, j: (ctx_row, 0, 0), lw, l,
                  T=FFN_ROWS).reshape(n_b, n_s, D_MODEL)
        ks.append(k.reshape(n_b, N_KV, HEAD_DIM, n_s))
        vs.append(v.reshape(n_b, N_KV, HEAD_DIM, n_s))
        xs = _mixer(xs, mod_mix, lambda b, j: (b, 0, 0), lw, l, T=512, ctx_k=ck, ctx_v=cv,
                    rope_tab=rope_tab)[0]
        xs = _ffn(xs, mod_ffn, lambda b, j: (b, 0, 0), lw, l, T=FFN_ROWS)
    new_k = jnp.stack(ks, axis=0).transpose(1, 0, 4, 2, 3)
    new_v = jnp.stack(vs, axis=0).transpose(1, 0, 4, 2, 3)
    return (xp, xs, new_k, new_v)
```

```python
import functools

import jax
import jax.numpy as jnp
import numpy as np
from jax import lax
from jax.experimental import pallas as pl
from jax.experimental.pallas import tpu as pltpu

F32 = jnp.float32
BF16 = jnp.bfloat16

D_MODEL = 1024
DEPTH = 2
GRID_W = 64
N_HEADS = 8
N_KV = 2
HEAD_DIM = 64
GROUP = N_HEADS // N_KV
ATT_W = N_HEADS * HEAD_DIM
KV_W = N_KV * HEAD_DIM
WINDOW = 128
ROPE_BASE = 10000.0
CONV_W = D_MODEL // 4
CONV_K = 31
POOL_W = D_MODEL // 4
POOL_SIZES = (2, 4, 8, 16)
POOL_G = POOL_W // len(POOL_SIZES)
SC_W = D_MODEL // 4
SC_K = 3
N_BRANCH = 4
D_FF = 4 * D_MODEL
EPS = 1e-6
NEG = -1e30
IN_W = 2 * CONV_W + POOL_W + 3 * SC_W + ATT_W + 2 * KV_W + N_BRANCH * D_MODEL

OFF_AVAL = 0
OFF_AGATE = OFF_AVAL + CONV_W
OFF_POOL = OFF_AGATE + CONV_W
OFF_SB = OFF_POOL + POOL_W
OFF_SC = OFF_SB + SC_W
OFF_SH = OFF_SC + SC_W
OFF_Q = OFF_SH + SC_W
OFF_K = OFF_Q + ATT_W
OFF_V = OFF_K + KV_W
OFF_G = OFF_V + KV_W

LANES = 128
SUBLANES = 8
BF16_ROWS = 16
VMEM_LIMIT_BYTES = 56 * 1024 * 1024

HALO_ATT = WINDOW
HALO_CONV = 16
FFN_ROWS = 1024
CTX_SEQS = 2
SEQ_SKEW = 24
ROWS_SPLIT = 256
ROWS_SKEW = 32
ROW_CHUNK = 64
COL_CHUNK = 512
MIX_CHUNK = 512
N_EARLY = 2
ADA_ROWS = 8
ADA_COLS = 1024


def _dot(a, b):
    return jnp.dot(a, b, preferred_element_type=F32)


def _dot_nt(a, b):
    return lax.dot_general(a, b, (((1,), (1,)), ((), ())), preferred_element_type=F32)


def _rms(x):
    return x * lax.rsqrt(jnp.mean(x * x, axis=-1, keepdims=True) + EPS)


def _sigmoid(x):
    return 0.5 * jnp.tanh(0.5 * x) + 0.5


def _ada_kernel(cond_ref, w_ref, b_ref, o_ref):
    c = cond_ref[...]
    s = (c * _sigmoid(c)).astype(BF16)
    o_ref[0] = _dot(s, w_ref[0].astype(BF16)) + b_ref[0]


def _ada(cond, w_ada, b_ada):
    n_col = w_ada.shape[-1] // ADA_COLS
    return pl.pallas_call(
        _ada_kernel,
        grid=(DEPTH, n_col),
        in_specs=[
            pl.BlockSpec((ADA_ROWS, D_MODEL), lambda l, c: (0, 0)),
            pl.BlockSpec((1, D_MODEL, ADA_COLS), lambda l, c: (l, 0, c)),
            pl.BlockSpec((1, 1, ADA_COLS), lambda l, c: (l, 0, c)),
        ],
        out_specs=pl.BlockSpec((1, ADA_ROWS, ADA_COLS), lambda l, c: (l, 0, c)),
        out_shape=jax.ShapeDtypeStruct((DEPTH, ADA_ROWS, w_ada.shape[-1]), F32),
        name="ada",
    )(cond, w_ada, b_ada.reshape(DEPTH, 1, -1))


def _mixer_kernel(*refs, names, per_seq, n_seq, T, L, halo):
    streams = []
    for s in range(n_seq):
        prefix, rows = _mixer_tile({nm: (ref.at[s] if nm in per_seq else ref) for nm, ref in zip(names, refs)},
                                   T, L, halo)
        split = min(ROWS_SPLIT, T)
        row_streams = [rows(lo, lo + split) for lo in range(0, T, split)]
        streams.append(_chain(prefix(), _skewed(row_streams, ROWS_SKEW)))
    for _ in _skewed(streams, SEQ_SKEW):
        pass


def _mixer_tile(r, T, L, halo):
    H = HALO_ATT if halo else 0
    E = T + 2 * H
    TC = T + 2 * HALO_CONV
    BQ = WINDOW if halo else T
    pos0 = pl.program_id(1) * T

    mod = r["mod"][0]
    sh1, sc1, gt1 = mod[0:1], mod[1:2], mod[2:3]
    g_pre = r["g_pre"][...]
    he = r["he"]
    w_in = r["w_in"]

    g_mod = g_pre * (1.0 + sc1)

    def norm_mod(x):
        return (_rms(x) * g_mod + sh1).astype(BF16)

    def norm_stage():
        if halo:
            he[0:H] = norm_mod(r["xp"][...])
            he[H + T:E] = norm_mod(r["xn"][...])
        for i in range(0, T, 256):
            he[H + i:H + i + 256] = norm_mod(r["xm"][i:i + 256])
        yield

    def he_rows(lo, hi):
        return he[H + lo:H + hi]

    ua, up, uc, uas = r["ua"], r["up"], r["uc"], r["uas"]

    def depthwise_inputs():
        if halo:
            c_lo, n_c, d_lo = H - HALO_CONV, TC, 0
            rp = pos0 - HALO_CONV + lax.broadcasted_iota(jnp.int32, (TC, 1), 0)
            in_seq = (rp >= 0) & (rp < L)
        else:
            c_lo, n_c, d_lo = 0, T, HALO_CONV
            zeros = jnp.zeros((HALO_CONV, CONV_W), F32)
            for buf in (ua, up, uc):
                buf[0:HALO_CONV] = zeros
                buf[HALO_CONV + T:TC] = zeros
        ua[TC:TC + SUBLANES] = jnp.zeros((SUBLANES, CONV_W), F32)

        def seq_mask(v):
            return jnp.where(in_seq, v, 0.0) if halo else v

        za = _dot(he[c_lo:c_lo + n_c], w_in[:, OFF_AVAL:OFF_AVAL + 2 * CONV_W])
        ua[d_lo:d_lo + n_c] = seq_mask(za[:, :CONV_W] * _sigmoid(za[:, CONV_W:]))
        yield
        zp = _dot(he[c_lo:c_lo + n_c], w_in[:, OFF_POOL:OFF_POOL + POOL_W])
        up[d_lo:d_lo + n_c] = seq_mask(zp)
        yield
        zs = _dot(he[c_lo:c_lo + n_c], w_in[:, OFF_SC:OFF_SC + 2 * SC_W])
        uc[d_lo:d_lo + n_c] = seq_mask(zs[:, :SC_W] * zs[:, SC_W:])
        yield
        for s in range(1, SUBLANES):
            uas[s - 1] = ua[s:s + TC]
        yield

    merged = r["merged"]
    act_a, act_p, act_b, act_c = r["act_a"], r["act_p"], r["act_b"], r["act_c"]
    M0 = HALO_CONV

    g_early = r["g_early"]

    def gate_logits(branch, c, lo, hi):
        g_off = OFF_G + branch * D_MODEL + c
        return _dot(he_rows(lo, hi), w_in[:, g_off:g_off + MIX_CHUNK])

    def gates_early(branch, lo, hi):
        for c in range(0, D_MODEL, MIX_CHUNK):
            g_early[branch, lo:hi, c:c + MIX_CHUNK] = gate_logits(branch, c, lo, hi)
            yield

    def gate_merge(branch, act, w_out, lo, hi):
        for c in range(0, D_MODEL, MIX_CHUNK):
            g = _sigmoid(g_early[branch, lo:hi, c:c + MIX_CHUNK] if branch < N_EARLY
                         else gate_logits(branch, c, lo, hi))
            val = g * _dot(act[lo:hi], w_out[:, c:c + MIX_CHUNK])
            if branch == 0:
                merged[lo:hi, c:c + MIX_CHUNK] = val
            else:
                merged[lo:hi, c:c + MIX_CHUNK] += val
            yield

    def conv_a(lo_row, hi_row):
        wca = r["w_conv_a"][...]
        b_conv, ln_g, ln_b = r["b_conv_a"][...], r["ln_g_a"][...], r["ln_b_a"][...]
        for t0 in range(lo_row, hi_row, ROW_CHUNK):
            acc = jnp.zeros((ROW_CHUNK, CONV_W), F32)
            for k in range(CONV_K):
                lo = M0 + t0 + k - CONV_K // 2
                s, base = lo % SUBLANES, lo - lo % SUBLANES
                win = ua[base:base + ROW_CHUNK] if s == 0 else uas[s - 1, base:base + ROW_CHUNK]
                acc = acc + wca[k:k + 1] * win
            acc = acc + b_conv
            mu = jnp.mean(acc, axis=-1, keepdims=True)
            cen = acc - mu
            var = jnp.mean(cen * cen, axis=-1, keepdims=True)
            y = (cen * lax.rsqrt(var + EPS)) * ln_g + ln_b
            act_a[t0:t0 + ROW_CHUNK] = (y * _sigmoid(y)).astype(BF16)
            yield

    def pool(lo_row, hi_row):
        lane = lax.broadcasted_iota(jnp.int32, (ROW_CHUNK, LANES), 1)
        low_half = lane < POOL_G
        for t0 in range(lo_row, hi_row, ROW_CHUNK):
            pos = pos0 + t0 + lax.broadcasted_iota(jnp.int32, (ROW_CHUNK, 1), 0)

            def inv_cnt(w):
                cnt = jnp.minimum(pos + w // 2, L) - jnp.maximum(pos - w // 2, 0)
                return 1.0 / cnt.astype(F32)

            def win(lt, off):
                lo = M0 + t0 + off
                return up[lo:lo + ROW_CHUNK, lt * LANES:(lt + 1) * LANES]

            for lt in range(POOL_W // LANES):
                w_small, w_big = POOL_SIZES[2 * lt], POOL_SIZES[2 * lt + 1]
                tok = win(lt, 0)
                s = win(lt, -1) + tok
                have = 2
                sums = {2: s}
                while have < w_big:
                    nxt = 2 * have
                    for o in range(have // 2, nxt // 2):
                        s = s + win(lt, -o - 1) + win(lt, o)
                    have = nxt
                    sums[have] = s
                pooled = jnp.where(low_half, sums[w_small] * inv_cnt(w_small),
                                   sums[w_big] * inv_cnt(w_big)) - tok
                act_p[t0:t0 + ROW_CHUNK, lt * LANES:(lt + 1) * LANES] = pooled.astype(BF16)
            yield

    def pool_map(lo, hi):
        ub = _dot(act_p[lo:hi], r["w_poolbd"][...]) * r["pool_scale"][...]
        act_b[lo:hi] = ub.astype(BF16)
        yield

    def conv_c(lo_row, hi_row):
        wsc = r["w_sc"][...]
        s_b = _dot(he_rows(lo_row, hi_row), w_in[:, OFF_SB:OFF_SB + SC_W])
        yield
        for t0 in range(lo_row, hi_row, ROW_CHUNK):
            acc = jnp.zeros((ROW_CHUNK, SC_W), F32)
            for k in range(SC_K):
                lo = M0 + t0 + k - SC_K // 2
                acc = acc + wsc[k:k + 1] * uc[lo:lo + ROW_CHUNK]
            act_c[t0:t0 + ROW_CHUNK] = (s_b[t0 - lo_row:t0 - lo_row + ROW_CHUNK] * acc).astype(BF16)
            yield

    def rope(x, tab):
        n = x.shape[0]
        ln = lax.broadcasted_iota(jnp.int32, (n, LANES), 1)
        first = (ln % (HEAD_DIM // 2)) < (HEAD_DIM // 4)
        quarter = HEAD_DIM // 4
        partner = jnp.where(first, pltpu.roll(x, LANES - quarter, 1), pltpu.roll(x, quarter, 1))
        return x * tab[:, :LANES] + partner * tab[:, LANES:]

    def dup_heads(x):
        ln = lax.broadcasted_iota(jnp.int32, x.shape, 1)
        sw = pltpu.roll(x, HEAD_DIM, 1)
        lo = ln < HEAD_DIM
        return jnp.where(lo, x, sw), jnp.where(lo, sw, x)

    kd, vd, q_s, o_s = r["kd"], r["vd"], r["q"], r["o"]

    def qkv_proj():
        kv = _dot(he[...], w_in[:, OFF_K:OFF_K + 2 * KV_W])
        k, v = kv[:, :KV_W], kv[:, KV_W:]
        if halo:
            tab_e = jnp.concatenate([r["rtp"][...], r["rtm"][...], r["rtn"][...]], axis=0)
            k = rope(k, tab_e)
        else:
            r["k_out"][...] = k.T
            r["v_out"][...] = v.T
        for g, (kk, vv) in enumerate(zip(dup_heads(k), dup_heads(v))):
            kd[g] = kk.astype(BF16)
            vd[g] = vv.astype(BF16)
        yield
        if halo:
            for g, (kk, vv) in enumerate(zip(dup_heads(r["ck"][...]), dup_heads(r["cv"][...]))):
                r["kcd"][g] = kk.astype(BF16)
                r["vcd"][g] = vv.astype(BF16)
            yield
        scale = HEAD_DIM ** -0.5
        for p in range(ATT_W // LANES):
            qp = _dot(he_rows(0, T), w_in[:, OFF_Q + p * LANES:OFF_Q + (p + 1) * LANES])
            if halo:
                qp = rope(qp, r["rtm"][...])
            q_s[:, p * LANES:(p + 1) * LANES] = (qp * scale).astype(BF16)
            yield

    sink = r["sink"]
    lane_q = lax.broadcasted_iota(jnp.int32, (BQ, LANES), 1) < HEAD_DIM
    n_loc = 3 * WINDOW if halo else T

    def scores(j, g):
        rows = slice(j * BQ, (j + 1) * BQ)
        q0 = q_s[rows, (2 * g) * LANES:(2 * g + 1) * LANES]
        q1 = q_s[rows, (2 * g + 1) * LANES:(2 * g + 2) * LANES]
        zero = jnp.zeros_like(q0)
        qz = jnp.concatenate([jnp.where(lane_q, q0, zero), jnp.where(lane_q, q1, zero),
                              jnp.where(lane_q, zero, q0), jnp.where(lane_q, zero, q1)], axis=0)
        k_loc = kd[g, j * BQ:j * BQ + n_loc] if halo else kd[g]
        return _dot_nt(qz, k_loc), (_dot_nt(qz, r["kcd"][g]) if halo else None)

    def attention(lo, hi):
        blocks = [(j, g) for j in range(lo // BQ, hi // BQ) for g in range(N_KV)]
        nxt = scores(*blocks[0])
        for i, (j, g) in enumerate(blocks):
            (s_loc, s_ctx), nxt = nxt, (scores(*blocks[i + 1]) if i + 1 < len(blocks) else None)
            rows = slice(j * BQ, (j + 1) * BQ)
            if halo:
                col = lax.broadcasted_iota(jnp.int32, (BQ, WINDOW), 1)
                row = lax.broadcasted_iota(jnp.int32, (BQ, WINDOW), 0)
                qpos0 = pos0 + j * BQ
                ok_prev = col >= row + jnp.where(qpos0 >= WINDOW, 0, WINDOW)
                ok_next = col <= row - jnp.where(qpos0 + 2 * WINDOW <= L, 0, WINDOW)
            heads = (4 * g, 4 * g + 2, 4 * g + 1, 4 * g + 3)
            v_loc = vd[g, j * BQ:j * BQ + n_loc] if halo else vd[g]
            p_loc, p_ctx, inv_l = [], [], []
            for b, h in enumerate(heads):
                rb = slice(b * BQ, (b + 1) * BQ)
                sl = s_loc[rb]
                if halo:
                    sl = jnp.concatenate([jnp.where(ok_prev, sl[:, :WINDOW], NEG), sl[:, WINDOW:2 * WINDOW],
                                          jnp.where(ok_next, sl[:, 2 * WINDOW:], NEG)], axis=1)
                sk = sink[h]
                m = jnp.maximum(jnp.max(sl, axis=-1, keepdims=True), sk)
                if halo:
                    sc = s_ctx[rb]
                    m = jnp.maximum(m, jnp.max(sc, axis=-1, keepdims=True))
                pl_ = jnp.exp(sl - m)
                den = jnp.sum(pl_, axis=-1, keepdims=True) + jnp.exp(sk - m)
                p_loc.append(pl_.astype(BF16))
                if halo:
                    pc = jnp.exp(sc - m)
                    den = den + jnp.sum(pc, axis=-1, keepdims=True)
                    p_ctx.append(pc.astype(BF16))
                inv_l.append(1.0 / den)
            o = _dot(jnp.concatenate(p_loc, axis=0), v_loc)
            if halo:
                o = o + _dot(jnp.concatenate(p_ctx, axis=0), r["vcd"][g])
            ob = [o[b * BQ:(b + 1) * BQ] * inv_l[b] for b in range(GROUP)]
            o_s[rows, (2 * g) * LANES:(2 * g + 1) * LANES] = jnp.where(lane_q, ob[0], ob[2]).astype(BF16)
            o_s[rows, (2 * g + 1) * LANES:(2 * g + 2) * LANES] = jnp.where(lane_q, ob[1], ob[3]).astype(BF16)
            yield

    def project_out(lo, hi):
        g_out = gt1 * r["g_post"][...]
        for i in range(lo, hi, 256):
            m = _dot(merged[i:i + 256].astype(BF16), r["w_o"][...])
            r["x1"][i:i + 256] = r["xm"][i:i + 256] + _rms(m) * g_out
            yield

    def prefix():
        yield from norm_stage()
        yield from depthwise_inputs()
        yield from qkv_proj()

    def rows(lo, hi):
        yield from _together(conv_a(lo, hi), *[gates_early(b, lo, hi) for b in range(N_EARLY)])
        yield from _together(pool(lo, hi), gate_merge(0, act_a, r["w_a_out"], lo, hi))
        yield from pool_map(lo, hi)
        yield from _together(conv_c(lo, hi), gate_merge(1, act_b, r["w_b_out"], lo, hi))
        yield from _together(attention(lo, hi), gate_merge(2, act_c, r["w_c_out"], lo, hi))
        yield from gate_merge(3, o_s, r["w_d_out"], lo, hi)
        yield from project_out(lo, hi)

    return prefix, rows


_DONE = object()


def _together(*stages):
    return _skewed(stages, 0)


def _skewed(stages, skew):
    waiting, live, step = list(stages), [], 0
    while waiting or live:
        while waiting and step >= skew * (len(stages) - len(waiting)):
            live.append(waiting.pop(0))
        for st in list(live):
            if next(st, _DONE) is _DONE:
                live.remove(st)
            else:
                yield
        step += 1


def _chain(*stages):
    for st in stages:
        yield from st


def _layer_spec(arr, l):
    nd = arr.ndim - 1
    return pl.BlockSpec((None,) + arr.shape[1:], lambda b, j: (l,) + (0,) * nd, pipeline_mode=pl.Buffered(1))


def _mixer(x, mod, mod_row, lw, l, *, T, n_seq=1, ctx_k=None, ctx_v=None, rope_tab=None):
    B, L, _ = x.shape
    halo = ctx_k is not None
    nt = L // T
    H = HALO_ATT if halo else 0
    E = T + 2 * H
    TC = T + 2 * HALO_CONV
    hb = T // HALO_ATT
    n_hb = L // HALO_ATT

    names, args, specs = [], [], []

    def add(name, arr, spec):
        names.append(name)
        args.append(arr)
        specs.append(spec)

    add("xm", x, pl.BlockSpec((n_seq, T, D_MODEL), lambda b, j: (b, j, 0)))
    if halo:
        add("xp", x, pl.BlockSpec((n_seq, HALO_ATT, D_MODEL), lambda b, j: (b, jnp.maximum(j * hb - 1, 0), 0)))
        add("xn", x, pl.BlockSpec((n_seq, HALO_ATT, D_MODEL),
                                  lambda b, j: (b, jnp.minimum((j + 1) * hb, n_hb - 1), 0)))
    add("mod", mod, pl.BlockSpec((1, 3, D_MODEL), mod_row))
    for nm in ("g_pre", "g_post", "w_in", "w_conv_a", "b_conv_a", "ln_g_a", "ln_b_a", "w_a_out", "w_poolbd",
               "pool_scale", "w_b_out", "w_sc", "w_c_out"):
        add(nm, lw[nm], _layer_spec(lw[nm], l))
    add("sink", lw["sink"][l], pl.BlockSpec(memory_space=pltpu.SMEM))
    for nm in ("w_d_out", "w_o"):
        add(nm, lw[nm], _layer_spec(lw[nm], l))
    if halo:
        kv_spec = pl.BlockSpec((n_seq, None) + ctx_k.shape[2:], lambda b, j: (b, l, 0, 0))
        add("ck", ctx_k, kv_spec)
        add("cv", ctx_v, kv_spec)
        add("rtm", rope_tab, pl.BlockSpec((T, 2 * LANES), lambda b, j: (j, 0)))
        add("rtp", rope_tab, pl.BlockSpec((HALO_ATT, 2 * LANES), lambda b, j: (jnp.maximum(j * hb - 1, 0), 0)))
        add("rtn", rope_tab, pl.BlockSpec((HALO_ATT, 2 * LANES),
                                          lambda b, j: (jnp.minimum((j + 1) * hb, n_hb - 1), 0)))

    out_names = ["x1"]
    out_shape = [jax.ShapeDtypeStruct((B, L, D_MODEL), F32)]
    out_specs = [pl.BlockSpec((n_seq, T, D_MODEL), lambda b, j: (b, j, 0))]
    if not halo:
        for nm in ("k_out", "v_out"):
            out_names.append(nm)
            out_shape.append(jax.ShapeDtypeStruct((B, KV_W, L), F32))
            out_specs.append(pl.BlockSpec((n_seq, KV_W, T), lambda b, j: (b, 0, j)))

    scratch = [("he", pltpu.VMEM((E, D_MODEL), BF16)),
               ("ua", pltpu.VMEM((TC + SUBLANES, CONV_W), F32)),
               ("uas", pltpu.VMEM((SUBLANES - 1, TC, CONV_W), F32)),
               ("up", pltpu.VMEM((TC, POOL_W), F32)),
               ("uc", pltpu.VMEM((TC, SC_W), F32)),
               ("act_a", pltpu.VMEM((T, CONV_W), BF16)),
               ("act_p", pltpu.VMEM((T, POOL_W), BF16)),
               ("act_b", pltpu.VMEM((T, POOL_W), BF16)),
               ("act_c", pltpu.VMEM((T, SC_W), BF16)),
               ("q", pltpu.VMEM((T, ATT_W), BF16)),
               ("kd", pltpu.VMEM((N_KV, E, LANES), BF16)),
               ("vd", pltpu.VMEM((N_KV, E, LANES), BF16)),
               ("o", pltpu.VMEM((T, ATT_W), BF16)),
               ("merged", pltpu.VMEM((T, D_MODEL), F32)),
               ("g_early", pltpu.VMEM((N_EARLY, T, D_MODEL), F32))]
    if halo:
        n_ctx = ctx_k.shape[2]
        scratch += [("kcd", pltpu.VMEM((N_KV, n_ctx, LANES), BF16)),
                    ("vcd", pltpu.VMEM((N_KV, n_ctx, LANES), BF16))]

    scratch = [(nm, pltpu.VMEM((n_seq,) + tuple(buf.shape), buf.dtype)) for nm, buf in scratch]
    per_seq = frozenset(["xm", "xp", "xn", "ck", "cv"] + out_names + [s[0] for s in scratch])
    all_names = tuple(names + out_names + [s[0] for s in scratch])
    kern = functools.partial(_mixer_kernel, names=all_names, per_seq=per_seq, n_seq=n_seq, T=T, L=L, halo=halo)
    return pl.pallas_call(
        kern,
        grid=(B // n_seq, nt),
        in_specs=specs,
        out_specs=out_specs,
        out_shape=out_shape,
        scratch_shapes=[s[1] for s in scratch],
        compiler_params=pltpu.CompilerParams(dimension_semantics=("arbitrary", "arbitrary"),
                                             vmem_limit_bytes=VMEM_LIMIT_BYTES),
        name="mixer_latent" if halo else "mixer_context",
    )(*args)


def _ffn_kernel(x_ref, mod_ref, g_pre_ref, g_post_ref, w1_ref, w2_ref, o_ref, h_ref, hid_ref):
    x = x_ref[0]
    mod = mod_ref[0]
    sh2, sc2, gt2 = mod[0:1], mod[1:2], mod[2:3]
    h_ref[...] = (_rms(x) * (g_pre_ref[...] * (1.0 + sc2)) + sh2).astype(BF16)
    for c in range(0, D_FF, COL_CHUNK):
        a = _dot(h_ref[...], w1_ref[:, c:c + COL_CHUNK])
        hid_ref[:, c:c + COL_CHUNK] = jnp.square(jnp.maximum(a, 0.0)).astype(BF16)
    f = _dot(hid_ref[...], w2_ref[...])
    o_ref[0] = x + _rms(f) * (gt2 * g_post_ref[...])


def _ffn(x, mod, mod_row, lw, l, *, T):
    B, L, _ = x.shape
    return pl.pallas_call(
        _ffn_kernel,
        grid=(B, L // T),
        in_specs=[
            pl.BlockSpec((1, T, D_MODEL), lambda b, j: (b, j, 0)),
            pl.BlockSpec((1, 3, D_MODEL), mod_row),
            _layer_spec(lw["g_pre_ffn"], l),
            _layer_spec(lw["g_post_ffn"], l),
            _layer_spec(lw["w_ff1"], l),
            _layer_spec(lw["w_ff2"], l),
        ],
        out_specs=pl.BlockSpec((1, T, D_MODEL), lambda b, j: (b, j, 0)),
        out_shape=jax.ShapeDtypeStruct(x.shape, F32),
        scratch_shapes=[pltpu.VMEM((T, D_MODEL), BF16), pltpu.VMEM((T, D_FF), BF16)],
        compiler_params=pltpu.CompilerParams(dimension_semantics=("arbitrary", "arbitrary"),
                                             vmem_limit_bytes=VMEM_LIMIT_BYTES),
        name="ffn",
    )(x, mod, lw["g_pre_ffn"], lw["g_post_ffn"], lw["w_ff1"], lw["w_ff2"])


def _rope_table(n):
    half = HEAD_DIM // 4
    freq = (np.float32(ROPE_BASE) ** (-np.arange(half, dtype=np.float32) / np.float32(half))).astype(np.float32)
    pos = np.arange(n)
    ang_r = ((pos // GRID_W).astype(np.float32)[:, None] * freq[None, :]).astype(np.float32)
    ang_c = ((pos % GRID_W).astype(np.float32)[:, None] * freq[None, :]).astype(np.float32)
    cos = np.concatenate([np.cos(ang_r)] * 2 + [np.cos(ang_c)] * 2, axis=-1)
    sin = np.concatenate([-np.sin(ang_r), np.sin(ang_r), -np.sin(ang_c), np.sin(ang_c)], axis=-1)
    return jnp.asarray(np.concatenate([cos, cos, sin, sin], axis=-1), dtype=F32)


def _block_diag(w_pool):
    n = w_pool.shape[0]
    rows = [jnp.concatenate([w_pool[g] if h == g else jnp.zeros_like(w_pool[g]) for h in range(n)], axis=1)
            for g in range(n)]
    return jnp.concatenate(rows, axis=0)


def kernel(x_prompt, x_sample, cache_k, cache_v, c, c_ctx, w_ada, b_ada, g_pre_mix, g_post_mix, g_pre_ffn,
           g_post_ffn, w_in, w_conv_a, b_conv_a, ln_g_a, ln_b_a, w_a_out, w_pool, pool_scale, w_b_out, w_sc,
           w_c_out, sink, w_d_out, w_o, w_ff1, w_ff2):
    n_dec = c.shape[0]
    ctx_row = n_dec
    cond = jnp.concatenate([c, c_ctx[None, :], jnp.zeros((ADA_ROWS - n_dec - 1, D_MODEL), F32)], axis=0)
    mod = _ada(cond, w_ada, b_ada).reshape(DEPTH, ADA_ROWS, 6, D_MODEL)

    def rows(a):
        return a.reshape(DEPTH, 1, -1)

    lw = {
        "g_pre": rows(g_pre_mix), "g_post": rows(g_post_mix),
        "g_pre_ffn": rows(g_pre_ffn), "g_post_ffn": rows(g_post_ffn),
        "w_in": w_in.astype(BF16),
        "w_conv_a": w_conv_a, "b_conv_a": rows(b_conv_a), "ln_g_a": rows(ln_g_a), "ln_b_a": rows(ln_b_a),
        "w_a_out": w_a_out.astype(BF16),
        "w_poolbd": jnp.stack([_block_diag(w_pool[l]) for l in range(DEPTH)]).astype(BF16),
        "pool_scale": rows(pool_scale), "w_b_out": w_b_out.astype(BF16),
        "w_sc": w_sc, "w_c_out": w_c_out.astype(BF16), "sink": sink,
        "w_d_out": w_d_out.astype(BF16), "w_o": w_o.astype(BF16),
        "w_ff1": w_ff1.astype(BF16), "w_ff2": w_ff2.astype(BF16),
    }
    rope_tab = _rope_table(x_sample.shape[1])
    n_b, n_s = x_prompt.shape[:2]
    ck = cache_k.reshape(cache_k.shape[:3] + (KV_W,))
    cv = cache_v.reshape(cache_v.shape[:3] + (KV_W,))

    xp, xs = x_prompt, x_sample
    ks, vs = [], []
    for l in range(DEPTH):
        mod_mix, mod_ffn = mod[l, :, 0:3], mod[l, :, 3:6]
        xp, k, v = _mixer(xp, mod_mix, lambda b, j: (ctx_row, 0, 0), lw, l, T=n_s, n_seq=CTX_SEQS)
        xp = _ffn(xp.reshape(-1, FFN_ROWS, D_MODEL), mod_ffn, lambda b, j: (ctx_row, 0, 0), lw, l,
                  T=FFN_ROWS).reshape(n_b, n_s, D_MODEL)
        ks.append(k.reshape(n_b, N_KV, HEAD_DIM, n_s))
        vs.append(v.reshape(n_b, N_KV, HEAD_DIM, n_s))
        xs = _mixer(xs, mod_mix, lambda b, j: (b, 0, 0), lw, l, T=512, ctx_k=ck, ctx_v=cv,
                    rope_tab=rope_tab)[0]
        xs = _ffn(xs, mod_ffn, lambda b, j: (b, 0, 0), lw, l, T=FFN_ROWS)
    new_k = jnp.stack(ks, axis=0).transpose(1, 0, 4, 2, 3)
    new_v = jnp.stack(vs, axis=0).transpose(1, 0, 4, 2, 3)
    return (xp, xs, new_k, new_v)
```

```python
import functools

import jax
import jax.numpy as jnp
import numpy as np
from jax import lax
from jax.experimental import pallas as pl
from jax.experimental.pallas import tpu as pltpu

F32 = jnp.float32
BF16 = jnp.bfloat16

D_MODEL = 1024
DEPTH = 2
GRID_W = 64
N_HEADS = 8
N_KV = 2
HEAD_DIM = 64
GROUP = N_HEADS // N_KV
ATT_W = N_HEADS * HEAD_DIM
KV_W = N_KV * HEAD_DIM
WINDOW = 128
ROPE_BASE = 10000.0
CONV_W = D_MODEL // 4
CONV_K = 31
POOL_W = D_MODEL // 4
POOL_SIZES = (2, 4, 8, 16)
POOL_G = POOL_W // len(POOL_SIZES)
SC_W = D_MODEL // 4
SC_K = 3
N_BRANCH = 4
D_FF = 4 * D_MODEL
EPS = 1e-6
NEG = -1e30
IN_W = 2 * CONV_W + POOL_W + 3 * SC_W + ATT_W + 2 * KV_W + N_BRANCH * D_MODEL

OFF_AVAL = 0
OFF_AGATE = OFF_AVAL + CONV_W
OFF_POOL = OFF_AGATE + CONV_W
OFF_SB = OFF_POOL + POOL_W
OFF_SC = OFF_SB + SC_W
OFF_SH = OFF_SC + SC_W
OFF_Q = OFF_SH + SC_W
OFF_K = OFF_Q + ATT_W
OFF_V = OFF_K + KV_W
OFF_G = OFF_V + KV_W

LANES = 128
SUBLANES = 8
BF16_ROWS = 16
VMEM_LIMIT_BYTES = 56 * 1024 * 1024

HALO_ATT = WINDOW
HALO_CONV = 16
FFN_ROWS = 1024
CTX_SEQS = 2
SEQ_SKEW = 24
ROWS_SPLIT = 256
ROWS_SKEW = 32
ROW_CHUNK = 64
COL_CHUNK = 512
MIX_CHUNK = 512
N_EARLY = 2
ADA_ROWS = 8
ADA_COLS = 1024


def _dot(a, b):
    return jnp.dot(a, b, preferred_element_type=F32)


def _dot_nt(a, b):
    return lax.dot_general(a, b, (((1,), (1,)), ((), ())), preferred_element_type=F32)


def _rms(x):
    return x * lax.rsqrt(jnp.mean(x * x, axis=-1, keepdims=True) + EPS)


def _sigmoid(x):
    return 0.5 * jnp.tanh(0.5 * x) + 0.5


def _ada_kernel(cond_ref, w_ref, b_ref, o_ref):
    c = cond_ref[...]
    s = (c * _sigmoid(c)).astype(BF16)
    o_ref[0] = _dot(s, w_ref[0].astype(BF16)) + b_ref[0]


def _ada(cond, w_ada, b_ada):
    n_col = w_ada.shape[-1] // ADA_COLS
    return pl.pallas_call(
        _ada_kernel,
        grid=(DEPTH, n_col),
        in_specs=[
            pl.BlockSpec((ADA_ROWS, D_MODEL), lambda l, c: (0, 0)),
            pl.BlockSpec((1, D_MODEL, ADA_COLS), lambda l, c: (l, 0, c)),
            pl.BlockSpec((1, 1, ADA_COLS), lambda l, c: (l, 0, c)),
        ],
        out_specs=pl.BlockSpec((1, ADA_ROWS, ADA_COLS), lambda l, c: (l, 0, c)),
        out_shape=jax.ShapeDtypeStruct((DEPTH, ADA_ROWS, w_ada.shape[-1]), F32),
        name="ada",
    )(cond, w_ada, b_ada.reshape(DEPTH, 1, -1))


def _mixer_kernel(*refs, names, per_seq, n_seq, T, L, halo):
    streams = []
    for s in range(n_seq):
        prefix, rows = _mixer_tile({nm: (ref.at[s] if nm in per_seq else ref) for nm, ref in zip(names, refs)},
                                   T, L, halo)
        split = min(ROWS_SPLIT, T)
        row_streams = [rows(lo, lo + split) for lo in range(0, T, split)]
        streams.append(_chain(prefix(), _skewed(row_streams, ROWS_SKEW)))
    for _ in _skewed(streams, SEQ_SKEW):
        pass


def _mixer_tile(r, T, L, halo):
    H = HALO_ATT if halo else 0
    E = T + 2 * H
    TC = T + 2 * HALO_CONV
    BQ = WINDOW
    pos0 = pl.program_id(1) * T

    mod = r["mod"][0]
    sh1, sc1, gt1 = mod[0:1], mod[1:2], mod[2:3]
    g_pre = r["g_pre"][...]
    he = r["he"]
    w_in = r["w_in"]

    g_mod = g_pre * (1.0 + sc1)

    def norm_mod(x):
        return (_rms(x) * g_mod + sh1).astype(BF16)

    def norm_stage():
        if halo:
            he[0:H] = norm_mod(r["xp"][...])
            he[H + T:E] = norm_mod(r["xn"][...])
        for i in range(0, T, 256):
            he[H + i:H + i + 256] = norm_mod(r["xm"][i:i + 256])
        yield

    def he_rows(lo, hi):
        return he[H + lo:H + hi]

    ua, up, uc, uas = r["ua"], r["up"], r["uc"], r["uas"]

    def depthwise_inputs():
        if halo:
            c_lo, n_c, d_lo = H - HALO_CONV, TC, 0
        else:
            c_lo, n_c, d_lo = 0, T, HALO_CONV
            zeros = jnp.zeros((HALO_CONV, CONV_W), F32)
            for buf in (ua, up, uc):
                buf[0:HALO_CONV] = zeros
                buf[HALO_CONV + T:TC] = zeros
        ua[TC:TC + SUBLANES] = jnp.zeros((SUBLANES, CONV_W), F32)

        src, dst = slice(c_lo, c_lo + n_c), slice(d_lo, d_lo + n_c)
        if halo:
            rp = pos0 - HALO_CONV + lax.broadcasted_iota(jnp.int32, (n_c, 1), 0)
            in_seq = (rp >= 0) & (rp < L)

        def seq_mask(v):
            return jnp.where(in_seq, v, 0.0) if halo else v

        za = _dot(he[src], w_in[:, OFF_AVAL:OFF_AVAL + 2 * CONV_W])
        ua[dst] = seq_mask(za[:, :CONV_W] * _sigmoid(za[:, CONV_W:]))
        yield
        zp = _dot(he[src], w_in[:, OFF_POOL:OFF_POOL + POOL_W])
        up[dst] = seq_mask(zp)
        yield
        zs = _dot(he[src], w_in[:, OFF_SC:OFF_SC + 2 * SC_W])
        uc[dst] = seq_mask(zs[:, :SC_W] * zs[:, SC_W:])
        yield
        for s in range(1, SUBLANES):
            uas[s - 1] = ua[s:s + TC]
        yield

    merged = r["merged"]
    act_a, act_p, act_b, act_c = r["act_a"], r["act_p"], r["act_b"], r["act_c"]
    M0 = HALO_CONV

    g_early = r["g_early"]

    def gate_logits(branch, c, lo, hi):
        g_off = OFF_G + branch * D_MODEL + c
        return _dot(he_rows(lo, hi), w_in[:, g_off:g_off + MIX_CHUNK])

    def gates_early(branch, lo, hi):
        for c in range(0, D_MODEL, MIX_CHUNK):
            g_early[branch, lo:hi, c:c + MIX_CHUNK] = gate_logits(branch, c, lo, hi)
            yield

    def gate_merge(branch, act, w_out, lo, hi):
        for c in range(0, D_MODEL, MIX_CHUNK):
            g = _sigmoid(g_early[branch, lo:hi, c:c + MIX_CHUNK] if branch < N_EARLY
                         else gate_logits(branch, c, lo, hi))
            val = g * _dot(act[lo:hi], w_out[:, c:c + MIX_CHUNK])
            if branch == 0:
                merged[lo:hi, c:c + MIX_CHUNK] = val
            else:
                merged[lo:hi, c:c + MIX_CHUNK] += val
            yield

    def conv_a(lo_row, hi_row):
        wca = r["w_conv_a"][...]
        b_conv, ln_g, ln_b = r["b_conv_a"][...], r["ln_g_a"][...], r["ln_b_a"][...]
        for t0 in range(lo_row, hi_row, ROW_CHUNK):
            acc = jnp.zeros((ROW_CHUNK, CONV_W), F32)
            for k in range(CONV_K):
                lo = M0 + t0 + k - CONV_K // 2
                s, base = lo % SUBLANES, lo - lo % SUBLANES
                win = ua[base:base + ROW_CHUNK] if s == 0 else uas[s - 1, base:base + ROW_CHUNK]
                acc = acc + wca[k:k + 1] * win
            acc = acc + b_conv
            mu = jnp.mean(acc, axis=-1, keepdims=True)
            cen = acc - mu
            var = jnp.mean(cen * cen, axis=-1, keepdims=True)
            y = (cen * lax.rsqrt(var + EPS)) * ln_g + ln_b
            act_a[t0:t0 + ROW_CHUNK] = (y * _sigmoid(y)).astype(BF16)
            yield

    def pool(lo_row, hi_row):
        lane = lax.broadcasted_iota(jnp.int32, (ROW_CHUNK, LANES), 1)
        low_half = lane < POOL_G
        for t0 in range(lo_row, hi_row, ROW_CHUNK):
            pos = pos0 + t0 + lax.broadcasted_iota(jnp.int32, (ROW_CHUNK, 1), 0)

            def inv_cnt(w):
                cnt = jnp.minimum(pos + w // 2, L) - jnp.maximum(pos - w // 2, 0)
                return 1.0 / cnt.astype(F32)

            def win(lt, off):
                lo = M0 + t0 + off
                return up[lo:lo + ROW_CHUNK, lt * LANES:(lt + 1) * LANES]

            for lt in range(POOL_W // LANES):
                w_small, w_big = POOL_SIZES[2 * lt], POOL_SIZES[2 * lt + 1]
                tok = win(lt, 0)
                s = win(lt, -1) + tok
                have = 2
                sums = {2: s}
                while have < w_big:
                    nxt = 2 * have
                    for o in range(have // 2, nxt // 2):
                        s = s + win(lt, -o - 1) + win(lt, o)
                    have = nxt
                    sums[have] = s
                pooled = jnp.where(low_half, sums[w_small] * inv_cnt(w_small),
                                   sums[w_big] * inv_cnt(w_big)) - tok
                act_p[t0:t0 + ROW_CHUNK, lt * LANES:(lt + 1) * LANES] = pooled.astype(BF16)
            yield

    def pool_map(lo, hi):
        ub = _dot(act_p[lo:hi], r["w_poolbd"][...]) * r["pool_scale"][...]
        act_b[lo:hi] = ub.astype(BF16)
        yield

    def conv_c(lo_row, hi_row):
        wsc = r["w_sc"][...]
        s_b = _dot(he_rows(lo_row, hi_row), w_in[:, OFF_SB:OFF_SB + SC_W])
        yield
        for t0 in range(lo_row, hi_row, ROW_CHUNK):
            acc = jnp.zeros((ROW_CHUNK, SC_W), F32)
            for k in range(SC_K):
                lo = M0 + t0 + k - SC_K // 2
                acc = acc + wsc[k:k + 1] * uc[lo:lo + ROW_CHUNK]
            act_c[t0:t0 + ROW_CHUNK] = (s_b[t0 - lo_row:t0 - lo_row + ROW_CHUNK] * acc).astype(BF16)
            yield

    def rope(x, tab):
        n = x.shape[0]
        ln = lax.broadcasted_iota(jnp.int32, (n, LANES), 1)
        first = (ln % (HEAD_DIM // 2)) < (HEAD_DIM // 4)
        quarter = HEAD_DIM // 4
        partner = jnp.where(first, pltpu.roll(x, LANES - quarter, 1), pltpu.roll(x, quarter, 1))
        return x * tab[:, :LANES] + partner * tab[:, LANES:]

    def dup_heads(x):
        ln = lax.broadcasted_iota(jnp.int32, x.shape, 1)
        sw = pltpu.roll(x, HEAD_DIM, 1)
        lo = ln < HEAD_DIM
        return jnp.where(lo, x, sw), jnp.where(lo, sw, x)

    kd, vd, q_s, o_s = r["kd"], r["vd"], r["q"], r["o"]

    def qkv_proj():
        kv = _dot(he[...], w_in[:, OFF_K:OFF_K + 2 * KV_W])
        k, v = kv[:, :KV_W], kv[:, KV_W:]
        if halo:
            tab_e = jnp.concatenate([r["rtp"][...], r["rtm"][...], r["rtn"][...]], axis=0)
            k = rope(k, tab_e)
        else:
            r["k_out"][...] = k.T
            r["v_out"][...] = v.T
        for g, (kk, vv) in enumerate(zip(dup_heads(k), dup_heads(v))):
            kd[g] = kk.astype(BF16)
            vd[g] = vv.astype(BF16)
        yield
        if halo:
            for g, (kk, vv) in enumerate(zip(dup_heads(r["ck"][...]), dup_heads(r["cv"][...]))):
                r["kcd"][g] = kk.astype(BF16)
                r["vcd"][g] = vv.astype(BF16)
            yield
        scale = HEAD_DIM ** -0.5
        for p in range(ATT_W // LANES):
            qp = _dot(he_rows(0, T), w_in[:, OFF_Q + p * LANES:OFF_Q + (p + 1) * LANES])
            if halo:
                qp = rope(qp, r["rtm"][...])
            q_s[:, p * LANES:(p + 1) * LANES] = (qp * scale).astype(BF16)
            yield

    sink = r["sink"]
    lane_q = lax.broadcasted_iota(jnp.int32, (BQ, LANES), 1) < HEAD_DIM
    n_loc = 3 * WINDOW if halo else T

    def scores(j, g):
        rows = slice(j * BQ, (j + 1) * BQ)
        q0 = q_s[rows, (2 * g) * LANES:(2 * g + 1) * LANES]
        q1 = q_s[rows, (2 * g + 1) * LANES:(2 * g + 2) * LANES]
        zero = jnp.zeros_like(q0)
        qz = jnp.concatenate([jnp.where(lane_q, q0, zero), jnp.where(lane_q, q1, zero),
                              jnp.where(lane_q, zero, q0), jnp.where(lane_q, zero, q1)], axis=0)
        k_loc = kd[g, j * BQ:j * BQ + n_loc] if halo else kd[g]
        return _dot_nt(qz, k_loc), (_dot_nt(qz, r["kcd"][g]) if halo else None)

    def attention(lo, hi):
        blocks = [(j, g) for j in range(lo // BQ, hi // BQ) for g in range(N_KV)]
        nxt = scores(*blocks[0])
        for i, (j, g) in enumerate(blocks):
            (s_loc, s_ctx), nxt = nxt, (scores(*blocks[i + 1]) if i + 1 < len(blocks) else None)
            rows = slice(j * BQ, (j + 1) * BQ)
            if halo:
                col = lax.broadcasted_iota(jnp.int32, (BQ, WINDOW), 1)
                row = lax.broadcasted_iota(jnp.int32, (BQ, WINDOW), 0)
                qpos0 = pos0 + j * BQ
                ok_prev = col >= row + jnp.where(qpos0 >= WINDOW, 0, WINDOW)
                ok_next = col <= row - jnp.where(qpos0 + 2 * WINDOW <= L, 0, WINDOW)
            heads = (4 * g, 4 * g + 2, 4 * g + 1, 4 * g + 3)
            v_loc = vd[g, j * BQ:j * BQ + n_loc] if halo else vd[g]
            p_loc, p_ctx, inv_l = [], [], []
            for b, h in enumerate(heads):
                rb = slice(b * BQ, (b + 1) * BQ)
                sl = s_loc[rb]
                if halo:
                    sl = jnp.concatenate([jnp.where(ok_prev, sl[:, :WINDOW], NEG), sl[:, WINDOW:2 * WINDOW],
                                          jnp.where(ok_next, sl[:, 2 * WINDOW:], NEG)], axis=1)
                sk = sink[h]
                m = jnp.maximum(jnp.max(sl, axis=-1, keepdims=True), sk)
                if halo:
                    sc = s_ctx[rb]
                    m = jnp.maximum(m, jnp.max(sc, axis=-1, keepdims=True))
                pl_ = jnp.exp(sl - m)
                den = jnp.sum(pl_, axis=-1, keepdims=True) + jnp.exp(sk - m)
                p_loc.append(pl_.astype(BF16))
                if halo:
                    pc = jnp.exp(sc - m)
                    den = den + jnp.sum(pc, axis=-1, keepdims=True)
                    p_ctx.append(pc.astype(BF16))
                inv_l.append(1.0 / den)
            o = _dot(jnp.concatenate(p_loc, axis=0), v_loc)
            if halo:
                o = o + _dot(jnp.concatenate(p_ctx, axis=0), r["vcd"][g])
            ob = [o[b * BQ:(b + 1) * BQ] * inv_l[b] for b in range(GROUP)]
            o_s[rows, (2 * g) * LANES:(2 * g + 1) * LANES] = jnp.where(lane_q, ob[0], ob[2]).astype(BF16)
            o_s[rows, (2 * g + 1) * LANES:(2 * g + 2) * LANES] = jnp.where(lane_q, ob[1], ob[3]).astype(BF16)
            yield

    def project_out(lo, hi):
        g_out = gt1 * r["g_post"][...]
        for i in range(lo, hi, 256):
            j = min(i + 256, hi)
            m = _dot(merged[i:j].astype(BF16), r["w_o"][...])
            r["x1"][i:j] = r["xm"][i:j] + _rms(m) * g_out
            yield

    def prefix():
        yield from norm_stage()
        yield from depthwise_inputs()
        yield from qkv_proj()

    def rows(lo, hi):
        yield from _together(conv_a(lo, hi), *[gates_early(b, lo, hi) for b in range(N_EARLY)])
        yield from _together(pool(lo, hi), gate_merge(0, act_a, r["w_a_out"], lo, hi))
        yield from pool_map(lo, hi)
        yield from _together(conv_c(lo, hi), gate_merge(1, act_b, r["w_b_out"], lo, hi))
        yield from _together(attention(lo, hi), gate_merge(2, act_c, r["w_c_out"], lo, hi))
        yield from gate_merge(3, o_s, r["w_d_out"], lo, hi)
        yield from project_out(lo, hi)

    return prefix, rows


_DONE = object()


def _together(*stages):
    return _skewed(stages, 0)


def _skewed(stages, skew):
    waiting, live, step = list(stages), [], 0
    while waiting or live:
        while waiting and step >= skew * (len(stages) - len(waiting)):
            live.append(waiting.pop(0))
        for st in list(live):
            if next(st, _DONE) is _DONE:
                live.remove(st)
            else:
                yield
        step += 1


def _chain(*stages):
    for st in stages:
        yield from st


def _layer_spec(arr, l):
    nd = arr.ndim - 1
    return pl.BlockSpec((None,) + arr.shape[1:], lambda b, j: (l,) + (0,) * nd, pipeline_mode=pl.Buffered(1))


def _mixer(x, mod, mod_row, lw, l, *, T, n_seq=1, ctx_k=None, ctx_v=None, rope_tab=None):
    B, L, _ = x.shape
    halo = ctx_k is not None
    nt = L // T
    H = HALO_ATT if halo else 0
    E = T + 2 * H
    TC = T + 2 * HALO_CONV
    hb = T // HALO_ATT
    n_hb = L // HALO_ATT

    names, args, specs = [], [], []

    def add(name, arr, spec):
        names.append(name)
        args.append(arr)
        specs.append(spec)

    add("xm", x, pl.BlockSpec((n_seq, T, D_MODEL), lambda b, j: (b, j, 0)))
    if halo:
        add("xp", x, pl.BlockSpec((n_seq, HALO_ATT, D_MODEL), lambda b, j: (b, jnp.maximum(j * hb - 1, 0), 0)))
        add("xn", x, pl.BlockSpec((n_seq, HALO_ATT, D_MODEL),
                                  lambda b, j: (b, jnp.minimum((j + 1) * hb, n_hb - 1), 0)))
    add("mod", mod, pl.BlockSpec((1, 3, D_MODEL), mod_row))
    for nm in ("g_pre", "g_post", "w_in", "w_conv_a", "b_conv_a", "ln_g_a", "ln_b_a", "w_a_out", "w_poolbd",
               "pool_scale", "w_b_out", "w_sc", "w_c_out"):
        add(nm, lw[nm], _layer_spec(lw[nm], l))
    add("sink", lw["sink"][l], pl.BlockSpec(memory_space=pltpu.SMEM))
    for nm in ("w_d_out", "w_o"):
        add(nm, lw[nm], _layer_spec(lw[nm], l))
    if halo:
        kv_spec = pl.BlockSpec((n_seq, None) + ctx_k.shape[2:], lambda b, j: (b, l, 0, 0))
        add("ck", ctx_k, kv_spec)
        add("cv", ctx_v, kv_spec)
        add("rtm", rope_tab, pl.BlockSpec((T, 2 * LANES), lambda b, j: (j, 0)))
        add("rtp", rope_tab, pl.BlockSpec((HALO_ATT, 2 * LANES), lambda b, j: (jnp.maximum(j * hb - 1, 0), 0)))
        add("rtn", rope_tab, pl.BlockSpec((HALO_ATT, 2 * LANES),
                                          lambda b, j: (jnp.minimum((j + 1) * hb, n_hb - 1), 0)))

    out_names = ["x1"]
    out_shape = [jax.ShapeDtypeStruct((B, L, D_MODEL), F32)]
    out_specs = [pl.BlockSpec((n_seq, T, D_MODEL), lambda b, j: (b, j, 0))]
    if not halo:
        for nm in ("k_out", "v_out"):
            out_names.append(nm)
            out_shape.append(jax.ShapeDtypeStruct((B, KV_W, L), F32))
            out_specs.append(pl.BlockSpec((n_seq, KV_W, T), lambda b, j: (b, 0, j)))

    scratch = [("he", pltpu.VMEM((E, D_MODEL), BF16)),
               ("ua", pltpu.VMEM((TC + SUBLANES, CONV_W), F32)),
               ("uas", pltpu.VMEM((SUBLANES - 1, TC, CONV_W), F32)),
               ("up", pltpu.VMEM((TC, POOL_W), F32)),
               ("uc", pltpu.VMEM((TC, SC_W), F32)),
               ("act_a", pltpu.VMEM((T, CONV_W), BF16)),
               ("act_p", pltpu.VMEM((T, POOL_W), BF16)),
               ("act_b", pltpu.VMEM((T, POOL_W), BF16)),
               ("act_c", pltpu.VMEM((T, SC_W), BF16)),
               ("q", pltpu.VMEM((T, ATT_W), BF16)),
               ("kd", pltpu.VMEM((N_KV, E, LANES), BF16)),
               ("vd", pltpu.VMEM((N_KV, E, LANES), BF16)),
               ("o", pltpu.VMEM((T, ATT_W), BF16)),
               ("merged", pltpu.VMEM((T, D_MODEL), F32)),
               ("g_early", pltpu.VMEM((N_EARLY, T, D_MODEL), F32))]
    if halo:
        n_ctx = ctx_k.shape[2]
        scratch += [("kcd", pltpu.VMEM((N_KV, n_ctx, LANES), BF16)),
                    ("vcd", pltpu.VMEM((N_KV, n_ctx, LANES), BF16))]

    scratch = [(nm, pltpu.VMEM((n_seq,) + tuple(buf.shape), buf.dtype)) for nm, buf in scratch]
    per_seq = frozenset(["xm", "xp", "xn", "ck", "cv"] + out_names + [s[0] for s in scratch])
    all_names = tuple(names + out_names + [s[0] for s in scratch])
    kern = functools.partial(_mixer_kernel, names=all_names, per_seq=per_seq, n_seq=n_seq, T=T, L=L, halo=halo)
    return pl.pallas_call(
        kern,
        grid=(B // n_seq, nt),
        in_specs=specs,
        out_specs=out_specs,
        out_shape=out_shape,
        scratch_shapes=[s[1] for s in scratch],
        compiler_params=pltpu.CompilerParams(dimension_semantics=("arbitrary", "arbitrary"),
                                             vmem_limit_bytes=VMEM_LIMIT_BYTES),
        name="mixer_latent" if halo else "mixer_context",
    )(*args)


def _ffn_kernel(x_ref, mod_ref, g_pre_ref, g_post_ref, w1_ref, w2_ref, o_ref, h_ref, hid_ref):
    x = x_ref[0]
    mod = mod_ref[0]
    sh2, sc2, gt2 = mod[0:1], mod[1:2], mod[2:3]
    h_ref[...] = (_rms(x) * (g_pre_ref[...] * (1.0 + sc2)) + sh2).astype(BF16)
    for c in range(0, D_FF, COL_CHUNK):
        a = _dot(h_ref[...], w1_ref[:, c:c + COL_CHUNK])
        hid_ref[:, c:c + COL_CHUNK] = jnp.square(jnp.maximum(a, 0.0)).astype(BF16)
    f = _dot(hid_ref[...], w2_ref[...])
    o_ref[0] = x + _rms(f) * (gt2 * g_post_ref[...])


def _ffn(x, mod, mod_row, lw, l, *, T):
    B, L, _ = x.shape
    return pl.pallas_call(
        _ffn_kernel,
        grid=(B, L // T),
        in_specs=[
            pl.BlockSpec((1, T, D_MODEL), lambda b, j: (b, j, 0)),
            pl.BlockSpec((1, 3, D_MODEL), mod_row),
            _layer_spec(lw["g_pre_ffn"], l),
            _layer_spec(lw["g_post_ffn"], l),
            _layer_spec(lw["w_ff1"], l),
            _layer_spec(lw["w_ff2"], l),
        ],
        out_specs=pl.BlockSpec((1, T, D_MODEL), lambda b, j: (b, j, 0)),
        out_shape=jax.ShapeDtypeStruct(x.shape, F32),
        scratch_shapes=[pltpu.VMEM((T, D_MODEL), BF16), pltpu.VMEM((T, D_FF), BF16)],
        compiler_params=pltpu.CompilerParams(dimension_semantics=("arbitrary", "arbitrary"),
                                             vmem_limit_bytes=VMEM_LIMIT_BYTES),
        name="ffn",
    )(x, mod, lw["g_pre_ffn"], lw["g_post_ffn"], lw["w_ff1"], lw["w_ff2"])


def _rope_table(n):
    half = HEAD_DIM // 4
    freq = (np.float32(ROPE_BASE) ** (-np.arange(half, dtype=np.float32) / np.float32(half))).astype(np.float32)
    pos = np.arange(n)
    ang_r = ((pos // GRID_W).astype(np.float32)[:, None] * freq[None, :]).astype(np.float32)
    ang_c = ((pos % GRID_W).astype(np.float32)[:, None] * freq[None, :]).astype(np.float32)
    cos = np.concatenate([np.cos(ang_r)] * 2 + [np.cos(ang_c)] * 2, axis=-1)
    sin = np.concatenate([-np.sin(ang_r), np.sin(ang_r), -np.sin(ang_c), np.sin(ang_c)], axis=-1)
    return jnp.asarray(np.concatenate([cos, cos, sin, sin], axis=-1), dtype=F32)


def _block_diag(w_pool):
    n = w_pool.shape[0]
    rows = [jnp.concatenate([w_pool[g] if h == g else jnp.zeros_like(w_pool[g]) for h in range(n)], axis=1)
            for g in range(n)]
    return jnp.concatenate(rows, axis=0)


def kernel(x_prompt, x_sample, cache_k, cache_v, c, c_ctx, w_ada, b_ada, g_pre_mix, g_post_mix, g_pre_ffn,
           g_post_ffn, w_in, w_conv_a, b_conv_a, ln_g_a, ln_b_a, w_a_out, w_pool, pool_scale, w_b_out, w_sc,
           w_c_out, sink, w_d_out, w_o, w_ff1, w_ff2):
    n_dec = c.shape[0]
    ctx_row = n_dec
    cond = jnp.concatenate([c, c_ctx[None, :], jnp.zeros((ADA_ROWS - n_dec - 1, D_MODEL), F32)], axis=0)
    mod = _ada(cond, w_ada, b_ada).reshape(DEPTH, ADA_ROWS, 6, D_MODEL)

    def rows(a):
        return a.reshape(DEPTH, 1, -1)

    lw = {
        "g_pre": rows(g_pre_mix), "g_post": rows(g_post_mix),
        "g_pre_ffn": rows(g_pre_ffn), "g_post_ffn": rows(g_post_ffn),
        "w_in": w_in.astype(BF16),
        "w_conv_a": w_conv_a, "b_conv_a": rows(b_conv_a), "ln_g_a": rows(ln_g_a), "ln_b_a": rows(ln_b_a),
        "w_a_out": w_a_out.astype(BF16),
        "w_poolbd": jnp.stack([_block_diag(w_pool[l]) for l in range(DEPTH)]).astype(BF16),
        "pool_scale": rows(pool_scale), "w_b_out": w_b_out.astype(BF16),
        "w_sc": w_sc, "w_c_out": w_c_out.astype(BF16), "sink": sink,
        "w_d_out": w_d_out.astype(BF16), "w_o": w_o.astype(BF16),
        "w_ff1": w_ff1.astype(BF16), "w_ff2": w_ff2.astype(BF16),
    }
    rope_tab = _rope_table(x_sample.shape[1])
    n_b, n_s = x_prompt.shape[:2]
    ck = cache_k.reshape(cache_k.shape[:3] + (KV_W,))
    cv = cache_v.reshape(cache_v.shape[:3] + (KV_W,))

    xp, xs = x_prompt, x_sample
    ks, vs = [], []
    for l in range(DEPTH):
        mod_mix, mod_ffn = mod[l, :, 0:3], mod[l, :, 3:6]
        xp, k, v = _mixer(xp, mod_mix, lambda b, j: (ctx_row, 0, 0), lw, l, T=n_s, n_seq=CTX_SEQS)
        xp = _ffn(xp.reshape(-1, FFN_ROWS, D_MODEL), mod_ffn, lambda b, j: (ctx_row, 0, 0), lw, l,
                  T=FFN_ROWS).reshape(n_b, n_s, D_MODEL)
        ks.append(k.reshape(n_b, N_KV, HEAD_DIM, n_s))
        vs.append(v.reshape(n_b, N_KV, HEAD_DIM, n_s))
        xs = _mixer(xs, mod_mix, lambda b, j: (b, 0, 0), lw, l, T=512, ctx_k=ck, ctx_v=cv,
                    rope_tab=rope_tab)[0]
        xs = _ffn(xs, mod_ffn, lambda b, j: (b, 0, 0), lw, l, T=FFN_ROWS)
    new_k = jnp.stack(ks, axis=0).transpose(1, 0, 4, 2, 3)
    new_v = jnp.stack(vs, axis=0).transpose(1, 0, 4, 2, 3)
    return (xp, xs, new_k, new_v)
```
